```python
import math
import jax, jax.numpy as jnp
from jax import lax
import numpy as np

D_MODEL = 2048
BATCH = 4
SEQ = 2048
DEPTH = 2
DEC_BATCH = 128
DEC_SEQ = 8
PAST_LEN = 16384
PAGE_SIZE = 128

MIX_WIDTH = D_MODEL
HG_HEADS = 8
HG_VDIM = MIX_WIDTH // 2
HG_DV = HG_VDIM // HG_HEADS
HG_DK = 128
HG_FDIM = HG_HEADS * HG_DK
SSM_DINNER = MIX_WIDTH - HG_VDIM
SSM_HEAD_DIM = 64
SSM_HEADS = SSM_DINNER // SSM_HEAD_DIM
SSM_GROUPS = 4
SSM_STATE = 128
SSM_GN = SSM_GROUPS * SSM_STATE
CONV_W = 4
CONV_DIM = SSM_DINNER + 2 * SSM_GN
CHUNK = 64
N_MEM = 256
X_HEADS = 4
X_HEAD_DIM = 128
X_DIM = X_HEADS * X_HEAD_DIM
D_FF = 4 * D_MODEL
EPS = 1e-6
IN_COLS = 2 * HG_FDIM + 2 * HG_VDIM + SSM_DINNER + CONV_DIM + SSM_HEADS
IN_SPLITS = (HG_FDIM, 2 * HG_FDIM, 2 * HG_FDIM + HG_VDIM, 2 * HG_FDIM + 2 * HG_VDIM,
             2 * HG_FDIM + 2 * HG_VDIM + SSM_DINNER, 2 * HG_FDIM + 2 * HG_VDIM + SSM_DINNER + CONV_DIM)

kernel_name = 'hymba_hgrn2_mamba2_memxattn_step'


def _rms(x, g):
    xf = x.astype(jnp.float32)
    y = xf * lax.rsqrt(jnp.mean(xf * xf, axis=-1, keepdims=True) + EPS)
    return (y * g.astype(jnp.float32)).astype(x.dtype)


def _chunk_len(L):
    return CHUNK if L % CHUNK == 0 else L


def _to_chunks(t, n, c):
    return t.reshape((t.shape[0], n, c) + t.shape[2:]).swapaxes(0, 1)


def _hgrn2(q_raw, f_raw, i_raw, g_raw, lb, onorm_g, s0):
    f32 = jnp.float32
    B, L, _ = q_raw.shape
    a = f_raw.astype(f32)
    lbf = lb.astype(f32)
    logf = jnp.logaddexp(jnp.log(lbf), jnp.log1p(-lbf) + jax.nn.log_sigmoid(a))
    k = (1.0 - lbf) * jax.nn.sigmoid(-a)
    q = jax.nn.silu(q_raw.astype(f32))
    v = i_raw.astype(f32)
    C = _chunk_len(L)
    n = L // C
    hd = lambda t, d: _to_chunks(t.reshape(B, L, HG_HEADS, d), n, C)
    xs = (hd(q, HG_DK), hd(k, HG_DK), hd(logf, HG_DK), hd(v, HG_DV))
    causal = jnp.tril(jnp.ones((C, C), dtype=bool))

    def step(S, inp):
        qc, kc, lc, vc = inp
        b = jnp.cumsum(lc, axis=1)
        o_inter = jnp.einsum('bthk,bhkv->bthv', qc * jnp.exp(b), S)
        diff = b[:, :, None] - b[:, None, :]
        decay = jnp.exp(jnp.where(causal[None, :, :, None, None], diff, -jnp.inf))
        A = jnp.einsum('bthk,bshk,btshk->bhts', qc, kc, decay)
        o = o_inter + jnp.einsum('bhts,bshv->bthv', A, vc)
        b_last = b[:, -1]
        k_dec = kc * jnp.exp(b_last[:, None] - b)
        S = jnp.exp(b_last)[..., None] * S + jnp.einsum('bshk,bshv->bhkv', k_dec, vc)
        return S, o

    S, o = lax.scan(step, s0.astype(f32), xs)
    o = o.swapaxes(0, 1).reshape(B, L, HG_HEADS, HG_DV)
    o = o * lax.rsqrt(jnp.mean(o * o, axis=-1, keepdims=True) + EPS)
    o = o.reshape(B, L, HG_VDIM) * onorm_g.astype(f32) * jax.nn.sigmoid(g_raw.astype(f32))
    return o.astype(q_raw.dtype), S.astype(s0.dtype)


def _mamba2(z, xbc, dt_raw, conv_w, conv_b, dt_bias, a_log, d_skip, norm_g, conv_st, h0):
    f32 = jnp.float32
    B, L, _ = xbc.shape
    xpad = jnp.concatenate([conv_st.astype(xbc.dtype), xbc], axis=1)
    new_conv = xpad[:, -(CONV_W - 1):]
    conv = conv_b.astype(f32)
    for j in range(CONV_W):
        conv = conv + xpad[:, j:j + L].astype(f32) * conv_w[j].astype(f32)
    u = jax.nn.silu(conv)
    xs = u[..., :SSM_DINNER].reshape(B, L, SSM_HEADS, SSM_HEAD_DIM)
    Bm = u[..., SSM_DINNER:SSM_DINNER + SSM_GN].reshape(B, L, SSM_GROUPS, SSM_STATE)
    Cm = u[..., SSM_DINNER + SSM_GN:].reshape(B, L, SSM_GROUPS, SSM_STATE)
    rep = SSM_HEADS // SSM_GROUPS
    Bh = jnp.repeat(Bm, rep, axis=2)
    Ch = jnp.repeat(Cm, rep, axis=2)
    dt = jax.nn.softplus(dt_raw.astype(f32) + dt_bias.astype(f32))
    a = dt * (-jnp.exp(a_log.astype(f32)))
    C = _chunk_len(L)
    n = L // C
    seq = (_to_chunks(xs, n, C), _to_chunks(Bh, n, C), _to_chunks(Ch, n, C),
           _to_chunks(dt, n, C), _to_chunks(a, n, C))
    causal = jnp.tril(jnp.ones((C, C), dtype=bool))

    def step(h, inp):
        xc, bc, cc, dtc, ac = inp
        cum = jnp.cumsum(ac, axis=1)
        y_inter = jnp.einsum('bthn,bhpn->bthp', cc, h) * jnp.exp(cum)[..., None]
        diff = cum[:, :, None] - cum[:, None, :]
        Lm = jnp.exp(jnp.where(causal[None, :, :, None], diff, -jnp.inf))
        scores = jnp.einsum('bthn,bshn->btsh', cc, bc) * Lm
        y = y_inter + jnp.einsum('btsh,bsh,bshp->bthp', scores, dtc, xc)
        cl = cum[:, -1]
        w = jnp.exp(cl[:, None] - cum) * dtc
        h = jnp.exp(cl)[..., None, None] * h + jnp.einsum('bsh,bshn,bshp->bhpn', w, bc, xc)
        return h, y

    h, y = lax.scan(step, h0.astype(f32), seq)
    y = y.swapaxes(0, 1).reshape(B, L, SSM_HEADS, SSM_HEAD_DIM)
    y = y + d_skip.astype(f32)[:, None] * xs
    y = y.reshape(B, L, SSM_DINNER) * jax.nn.silu(z.astype(f32))
    yg = y.reshape(B, L, SSM_GROUPS, SSM_DINNER // SSM_GROUPS)
    yg = yg * lax.rsqrt(jnp.mean(yg * yg, axis=-1, keepdims=True) + EPS)
    y = yg.reshape(B, L, SSM_DINNER) * norm_g.astype(f32)
    return y.astype(z.dtype), h.astype(h0.dtype), new_conv.astype(conv_st.dtype)


def _cross_attn(h, mk, mv, wq, wo):
    B, L, _ = h.shape
    q = (h @ wq).reshape(B, L, X_HEADS, X_HEAD_DIM)
    s = jnp.einsum('blhd,bmhd->bhlm', q, mk).astype(jnp.float32) * (1.0 / math.sqrt(X_HEAD_DIM))
    p = jax.nn.softmax(s, axis=-1).astype(h.dtype)
    o = jnp.einsum('bhlm,bmhd->blhd', p, mv).reshape(B, L, X_DIM)
    return o @ wo


def _layer(x, mk, mv, s_hg, h_ssm, conv_st, lb,
           g_pre_mix, g_post_mix, g_pre_x, g_post_x, g_pre_mlp, g_post_mlp,
           w_in, hg_onorm, conv_w, conv_b, dt_bias, a_log, d_skip, ssm_norm, w_out,
           wq_x, wo_x, w_mlp1, w_mlp2):
    h = _rms(x, g_pre_mix)
    proj = h @ w_in
    q_r, f_r, i_r, g_r, z, xbc, dt_r = jnp.split(proj, IN_SPLITS, axis=-1)
    o_hg, s_hg = _hgrn2(q_r, f_r, i_r, g_r, lb, hg_onorm, s_hg)
    o_ssm, h_ssm, conv_st = _mamba2(z, xbc, dt_r, conv_w, conv_b, dt_bias, a_log, d_skip,
                                    ssm_norm, conv_st, h_ssm)
    mix = jnp.concatenate([o_hg, o_ssm], axis=-1) @ w_out
    x = x + _rms(mix, g_post_mix)
    x = x + _rms(_cross_attn(_rms(x, g_pre_x), mk, mv, wq_x, wo_x), g_post_x)
    u = _rms(x, g_pre_mlp) @ w_mlp1
    x = x + _rms(jnp.square(jax.nn.relu(u)) @ w_mlp2, g_post_mlp)
    return x, s_hg, h_ssm, conv_st


def setup_inputs(seed: int = 0) -> dict:
    key = jax.random.key(seed)
    ks = jax.random.split(key, 40)
    f32 = jnp.float32
    nrm = lambda k, shape, s: jax.random.normal(k, shape, f32) * s
    gain = lambda k, shape: 1.0 + 0.01 * jax.random.normal(k, shape, f32)
    dt0 = jnp.exp(jax.random.uniform(ks[30], (DEPTH, SSM_HEADS), f32, math.log(1e-3), math.log(1e-1)))
    dt_bias = dt0 + jnp.log(-jnp.expm1(-dt0))
    a_log = jnp.log(jax.random.uniform(ks[31], (DEPTH, SSM_HEADS), f32, 1.0, 16.0))
    return {
        'x_prompt': nrm(ks[0], (BATCH, SEQ, D_MODEL), 1.0),
        'x_sample': nrm(ks[1], (DEC_BATCH, DEC_SEQ, D_MODEL), 1.0),
        'mem_prompt': nrm(ks[2], (BATCH, N_MEM, D_MODEL), 1.0),
        'state_hgrn': nrm(ks[3], (DEPTH, DEC_BATCH, HG_HEADS, HG_DK, HG_DV), 0.5),
        'state_ssm': nrm(ks[4], (DEPTH, DEC_BATCH, SSM_HEADS, SSM_HEAD_DIM, SSM_STATE), 0.1),
        'state_conv': nrm(ks[5], (DEPTH, DEC_BATCH, CONV_W - 1, CONV_DIM), 1.0),
        'cache_mem_k': nrm(ks[6], (DEPTH, DEC_BATCH, N_MEM, X_HEADS, X_HEAD_DIM), 1.0),
        'cache_mem_v': nrm(ks[7], (DEPTH, DEC_BATCH, N_MEM, X_HEADS, X_HEAD_DIM), 1.0),
        'g_pre_mix': gain(ks[8], (DEPTH, D_MODEL)),
        'g_post_mix': gain(ks[9], (DEPTH, D_MODEL)),
        'g_pre_x': gain(ks[10], (DEPTH, D_MODEL)),
        'g_post_x': gain(ks[11], (DEPTH, D_MODEL)),
        'g_pre_mlp': gain(ks[12], (DEPTH, D_MODEL)),
        'g_post_mlp': gain(ks[13], (DEPTH, D_MODEL)),
        'g_mem': gain(ks[14], (DEPTH, D_MODEL)),
        'g_final': gain(ks[15], (D_MODEL,)),
        'w_in': nrm(ks[16], (DEPTH, D_MODEL, IN_COLS), D_MODEL ** -0.5),
        'hg_lb': nrm(ks[17], (DEPTH, HG_FDIM), 1.0),
        'hg_onorm': gain(ks[18], (DEPTH, HG_VDIM)),
        'conv_w': nrm(ks[19], (DEPTH, CONV_W, CONV_DIM), CONV_W ** -0.5),
        'conv_b': nrm(ks[20], (DEPTH, CONV_DIM), 0.01),
        'dt_bias': dt_bias,
        'a_log': a_log,
        'd_skip': gain(ks[21], (DEPTH, SSM_HEADS)),
        'ssm_norm': gain(ks[22], (DEPTH, SSM_DINNER)),
        'w_out': nrm(ks[23], (DEPTH, MIX_WIDTH, D_MODEL), MIX_WIDTH ** -0.5),
        'wq_x': nrm(ks[24], (DEPTH, D_MODEL, X_DIM), D_MODEL ** -0.5),
        'wk_x': nrm(ks[25], (DEPTH, D_MODEL, X_DIM), D_MODEL ** -0.5),
        'wv_x': nrm(ks[26], (DEPTH, D_MODEL, X_DIM), D_MODEL ** -0.5),
        'wo_x': nrm(ks[27], (DEPTH, X_DIM, D_MODEL), X_DIM ** -0.5),
        'w_mlp1': nrm(ks[28], (DEPTH, D_MODEL, D_FF), D_MODEL ** -0.5),
        'w_mlp2': nrm(ks[29], (DEPTH, D_FF, D_MODEL), D_FF ** -0.5),
    }


def reference(x_prompt, x_sample, mem_prompt, state_hgrn, state_ssm, state_conv, cache_mem_k, cache_mem_v,
              g_pre_mix, g_post_mix, g_pre_x, g_post_x, g_pre_mlp, g_post_mlp, g_mem, g_final,
              w_in, hg_lb, hg_onorm, conv_w, conv_b, dt_bias, a_log, d_skip, ssm_norm, w_out,
              wq_x, wk_x, wv_x, wo_x, w_mlp1, w_mlp2):
    lb_cum = jnp.cumsum(jax.nn.softmax(hg_lb.astype(jnp.float32), axis=0), axis=0)
    lower_bounds = lb_cum - lb_cum[0:1]
    bp = x_prompt.shape[0]
    n_mem = mem_prompt.shape[1]
    dtp = x_prompt.dtype
    yp, ys = x_prompt, x_sample
    p_hg, p_ssm, p_conv, p_mk, p_mv = [], [], [], [], []
    s_hg, s_ssm, s_conv = [], [], []
    for l in range(DEPTH):
        w_l = (g_pre_mix[l], g_post_mix[l], g_pre_x[l], g_post_x[l], g_pre_mlp[l], g_post_mlp[l],
               w_in[l], hg_onorm[l], conv_w[l], conv_b[l], dt_bias[l], a_log[l], d_skip[l], ssm_norm[l],
               w_out[l], wq_x[l], wo_x[l], w_mlp1[l], w_mlp2[l])
        mem_n = _rms(mem_prompt, g_mem[l])
        mk = (mem_n @ wk_x[l]).reshape(bp, n_mem, X_HEADS, X_HEAD_DIM)
        mv = (mem_n @ wv_x[l]).reshape(bp, n_mem, X_HEADS, X_HEAD_DIM)
        z_hg = jnp.zeros((bp, HG_HEADS, HG_DK, HG_DV), dtp)
        z_ssm = jnp.zeros((bp, SSM_HEADS, SSM_HEAD_DIM, SSM_STATE), dtp)
        z_conv = jnp.zeros((bp, CONV_W - 1, CONV_DIM), dtp)
        yp, a1, a2, a3 = _layer(yp, mk, mv, z_hg, z_ssm, z_conv, lower_bounds[l], *w_l)
        ys, b1, b2, b3 = _layer(ys, cache_mem_k[l], cache_mem_v[l], state_hgrn[l], state_ssm[l],
                                state_conv[l], lower_bounds[l], *w_l)
        p_hg.append(a1); p_ssm.append(a2); p_conv.append(a3); p_mk.append(mk); p_mv.append(mv)
        s_hg.append(b1); s_ssm.append(b2); s_conv.append(b3)
    y_prompt = _rms(yp, g_final)
    y_sample = _rms(ys, g_final)
    return (y_prompt, y_sample,
            jnp.stack(p_hg), jnp.stack(p_ssm), jnp.stack(p_conv), jnp.stack(p_mk), jnp.stack(p_mv),
            jnp.stack(s_hg), jnp.stack(s_ssm), jnp.stack(s_conv))
```

```python
import functools
import math

import jax
import jax.numpy as jnp
from jax import lax
from jax.experimental import pallas as pl
from jax.experimental.pallas import tpu as pltpu

F32 = jnp.float32
BF16 = jnp.bfloat16
EPS = 1e-6

LANES = 128
BF16_ROWS = 16
VMEM_LIMIT = 56 * 1024 * 1024

HG_HEADS = 8
HG_DIM = 128
SSM_HEADS = 16
SSM_HEAD_DIM = 64
SSM_GROUPS = 4
SSM_STATE = 128
SSM_PAIRS = SSM_HEADS // 2
CONV_W = 4
X_HEADS = 4
X_HEAD_DIM = 128


def _dot(a, b):
    return jnp.dot(a, b, preferred_element_type=F32)


def _dot_nt(a, b):
    return lax.dot_general(a, b, (((1,), (1,)), ((), ())), preferred_element_type=F32)


def _dot_tn(a, b):
    return lax.dot_general(a, b, (((0,), (0,)), ((), ())), preferred_element_type=F32)


def _split3(x):
    x1 = x.astype(BF16)
    r = x - x1.astype(F32)
    x2 = r.astype(BF16)
    r = r - x2.astype(F32)
    return x1, x2, r.astype(BF16)


def _dot3(m_b, xs):
    return _dot(m_b, xs[0]) + _dot(m_b, xs[1]) + _dot(m_b, xs[2])


def _rms(x, g):
    ms = jnp.mean(x * x, axis=-1, keepdims=True)
    return x * lax.rsqrt(ms + EPS) * g


def _sigmoid(x):
    return jax.nn.sigmoid(x)


def _log_sigmoid(x):
    return jnp.minimum(x, 0.0) - jnp.log1p(jnp.exp(-jnp.abs(x)))


def _softplus(x):
    return jnp.maximum(x, 0.0) + jnp.log1p(jnp.exp(-jnp.abs(x)))


def _onehot_bf16(mask):
    return jnp.where(mask, 1.0, 0.0).astype(BF16)


def _segment_masks(rows, seg):
    shift = seg.bit_length() - 1
    r = lax.broadcasted_iota(jnp.int32, (rows, rows), 0)
    c = lax.broadcasted_iota(jnp.int32, (rows, rows), 1)
    same = lax.shift_right_logical(r, shift) == lax.shift_right_logical(c, shift)
    return same, same & (c <= r)


def _lane_column(row):
    p1, p2, p3 = _split3(row)
    r = lax.broadcasted_iota(jnp.int32, (BF16_ROWS, LANES), 0)
    stacked = jnp.where(r == 0, p1.astype(F32),
                        jnp.where(r == 1, p2.astype(F32),
                                  jnp.where(r == 2, p3.astype(F32), 0.0))).astype(BF16)
    return _dot_tn(stacked, jnp.ones((BF16_ROWS, LANES), BF16))


def _rms_matmul_kernel(x_ref, g_ref, w_ref, o_ref, xn_ref):
    @pl.when(pl.program_id(1) == 0)
    def _():
        xn_ref[...] = _rms(x_ref[...], g_ref[...]).astype(BF16)

    o_ref[...] = _dot(xn_ref[...], w_ref[...].astype(BF16))


def _in_proj_kernel(x_ref, g_ref, w_ref, wt_ref, o_ref, ot_ref, xn_ref, *, n_tail):
    @pl.when(pl.program_id(1) == 0)
    def _():
        xn = _rms(x_ref[...], g_ref[...]).astype(BF16)
        xn_ref[...] = xn
        lane = lax.broadcasted_iota(jnp.int32, wt_ref.shape, 1)
        wt = jnp.where(lane < n_tail, wt_ref[...], 0.0).astype(BF16)
        ot_ref[...] = _dot(xn, wt)

    o_ref[...] = _dot(xn_ref[...], w_ref[...].astype(BF16))


def _matmul_res_kernel(a_ref, w_ref, x_ref, g_ref, o_ref):
    k = pl.program_id(1)
    part = _dot(a_ref[...].astype(BF16), w_ref[...].astype(BF16))

    @pl.when(k == 0)
    def _():
        o_ref[...] = part

    @pl.when(k > 0)
    def _():
        o_ref[...] += part

    @pl.when(k == pl.num_programs(1) - 1)
    def _():
        o_ref[...] = x_ref[...] + _rms(o_ref[...], g_ref[...])


def _mlp_kernel(x_ref, g1_ref, w1_ref, w2_ref, g2_ref, o_ref, xn_ref):
    f = pl.program_id(1)

    @pl.when(f == 0)
    def _():
        xn_ref[...] = _rms(x_ref[...], g1_ref[...]).astype(BF16)

    h = _dot(xn_ref[...], w1_ref[...].astype(BF16))
    h = jnp.square(jnp.maximum(h, 0.0)).astype(BF16)
    part = _dot(h, w2_ref[...].astype(BF16))

    @pl.when(f == 0)
    def _():
        o_ref[...] = part

    @pl.when(f > 0)
    def _():
        o_ref[...] += part

    @pl.when(f == pl.num_programs(1) - 1)
    def _():
        o_ref[...] = x_ref[...] + _rms(o_ref[...], g2_ref[...])


def _final_rms_kernel(x_ref, g_ref, o_ref):
    o_ref[...] = _rms(x_ref[...], g_ref[...])


def _lower_bound(lb_all, layer):
    depth = lb_all.shape[0]
    rows = [lb_all[i] for i in range(depth)]
    m = functools.reduce(jnp.maximum, rows)
    es = [jnp.exp(r - m) for r in rows]
    tot = functools.reduce(lambda a, b: a + b, es)
    lb = jnp.zeros_like(rows[0])
    for i in range(1, layer + 1):
        lb = lb + es[i] / tot
    return lb


def _hgrn_kernel(*refs, layer, seg, carry, n_alias):
    q_ref, f_ref, i_ref, g_ref, lb_ref, on_ref = refs[:6]
    st_in_ref = None if carry else refs[6]
    rest = refs[(6 if carry else 7) + n_alias:]
    o_ref, st_out_ref = rest[:2]
    scratch = rest[2:]
    s_scr = None
    if carry:
        s_scr, *scratch = scratch
    qt_scr, kd_scr, dec_scr, oi_scr, kpad, bpad, vpad = scratch

    rows, width = q_ref.shape
    heads = width // LANES
    pad = kpad.shape[0] - rows

    fr = f_ref[...]
    qr = q_ref[...]
    lb = _lower_bound(lb_ref[...], layer)
    l1 = jnp.log(lb)
    l2 = jnp.log1p(-lb) + _log_sigmoid(fr)
    logf = jnp.maximum(l1, l2) + jnp.log1p(jnp.exp(-jnp.abs(l1 - l2)))
    k = (1.0 - lb) * _sigmoid(-fr)
    q = qr * _sigmoid(qr)
    v = i_ref[...]

    same, tri = _segment_masks(rows, seg)
    lf3 = _split3(logf)
    b = _dot3(_onehot_bf16(tri), lf3)
    bt = _dot3(_onehot_bf16(same), lf3)
    qt_scr[...] = q * jnp.exp(b)
    kd_scr[...] = k * jnp.exp(bt - b)
    dec_scr[...] = jnp.exp(bt)

    zpad = jnp.zeros((pad, width), F32)
    kpad[pl.ds(0, pad), :] = zpad
    bpad[pl.ds(0, pad), :] = zpad
    vpad[pl.ds(0, pad), :] = zpad
    kpad[pl.ds(pad, rows), :] = k
    bpad[pl.ds(pad, rows), :] = b
    vpad[pl.ds(pad, rows), :] = v

    tmod = lax.broadcasted_iota(jnp.int32, (rows, width), 0) & (seg - 1)
    ones_b = jnp.ones((LANES, LANES), BF16)
    acc = jnp.zeros((rows, width), F32)
    for d in range(seg):
        ks = kpad[pl.ds(pad - d, rows), :]
        bs = bpad[pl.ds(pad - d, rows), :]
        vs = vpad[pl.ds(pad - d, rows), :]
        e = jnp.where(tmod >= d, jnp.exp(b - bs), 0.0)
        prod = (q * ks * e).astype(BF16)
        w = jnp.concatenate(
            [_dot(prod[:, h * LANES:(h + 1) * LANES], ones_b) for h in range(heads)], axis=1)
        acc = acc + w * vs

    if carry:
        @pl.when(pl.program_id(1) == 0)
        def _():
            s_scr[...] = jnp.zeros_like(s_scr)

    def seg_body(n, c):
        r0 = pl.multiple_of(n * seg, seg)
        for h in range(heads):
            cs = slice(h * LANES, (h + 1) * LANES)
            s = s_scr[h] if carry else st_in_ref[n, h]
            qt_n = qt_scr[pl.ds(r0, seg), cs].astype(BF16)
            oi_scr[pl.ds(r0, seg), cs] = _dot(qt_n, s.astype(BF16))
            kd_n = kd_scr[pl.ds(r0, seg), cs].astype(BF16)
            v_n = vpad[pl.ds(pad + r0, seg), cs].astype(BF16)
            upd = _dot_tn(kd_n, v_n)
            s_new = _lane_column(dec_scr[pl.ds(r0, 1), cs]) * s + upd
            if carry:
                s_scr[h] = s_new
            else:
                st_out_ref[n, h] = s_new
        return c

    lax.fori_loop(0, rows // seg, seg_body, 0)

    o = acc + oi_scr[...]
    o = jnp.concatenate(
        [(lambda oh: oh * lax.rsqrt(jnp.mean(oh * oh, axis=-1, keepdims=True) + EPS))(
            o[:, h * LANES:(h + 1) * LANES]) for h in range(heads)], axis=1)
    o_ref[...] = o * on_ref[...] * _sigmoid(g_ref[...])

    if carry:
        @pl.when(pl.program_id(1) == pl.num_programs(1) - 1)
        def _():
            st_out_ref[...] = s_scr[...]


def _ssd_kernel(*refs, seg, carry, n_alias):
    z_ref, xa_ref, xb_ref, dt_ref, cw_ref, cb_ref, hp_ref, dsk_ref, ng_ref = refs[:9]
    st_in_ref, cs_in_ref = (None, None) if carry else refs[9:11]
    rest = refs[(9 if carry else 11) + n_alias:]
    o_ref, st_out_ref, cs_out_ref = rest[:3]
    scratch = rest[3:]
    h_scr = None
    if carry:
        h_scr, *scratch = scratch
    xpad, cm_scr, bm_scr, xw_scr, cle_scr, yi_scr = scratch

    rows, dinner = z_ref.shape
    cdim = xa_ref.shape[1] + xb_ref.shape[1]
    nseg = rows // seg
    gn = (cdim - dinner) // 2
    head_pad = xpad.shape[1] - seg
    tail = CONV_W - 1

    if carry:
        @pl.when(pl.program_id(1) == 0)
        def _():
            xpad[:, pl.ds(head_pad - tail, tail), :] = jnp.zeros((nseg, tail, cdim), F32)
    else:
        xpad[:, pl.ds(head_pad - tail, tail), :] = cs_in_ref[...]
    xpad[:, pl.ds(head_pad, seg), pl.ds(0, dinner)] = xa_ref[...].reshape(nseg, seg, dinner)
    xpad[:, pl.ds(head_pad, seg), pl.ds(dinner, cdim - dinner)] = xb_ref[...].reshape(nseg, seg, cdim - dinner)
    cw = cw_ref[...]
    conv = cb_ref[...] + xpad[:, pl.ds(head_pad - tail, seg), :] * cw[0:1, :]
    for j in range(1, CONV_W):
        conv = conv + xpad[:, pl.ds(head_pad - tail + j, seg), :] * cw[j:j + 1, :]
    new_tail = xpad[:, pl.ds(head_pad + seg - tail, tail), :]
    cs_out_ref[...] = new_tail.reshape(cs_out_ref.shape)
    if carry:
        xpad[:, pl.ds(head_pad - tail, tail), :] = new_tail
    u = conv.reshape(rows, cdim)
    u = u * _sigmoid(u)
    xs = u[:, :dinner]
    bm = u[:, dinner:dinner + gn]
    cm = u[:, dinner + gn:]

    hp = hp_ref[...]
    dt = _softplus(dt_ref[...] + hp[0:1, :])
    a = dt * (-jnp.exp(hp[1:2, :]))
    same, tri = _segment_masks(rows, seg)
    a3 = _split3(a)
    cum = _dot3(_onehot_bf16(tri), a3)
    cl = _dot3(_onehot_bf16(same), a3)
    cum3 = _split3(cum)
    dt3 = _split3(dt)
    hh = lax.broadcasted_iota(jnp.int32, (BF16_ROWS, LANES), 0)
    hl = lax.broadcasted_iota(jnp.int32, (BF16_ROWS, LANES), 1)
    eye_b = _onehot_bf16(hh == hl)
    cum_t = _dot_nt(eye_b, cum3[0]) + _dot_nt(eye_b, cum3[1]) + _dot_nt(eye_b, cum3[2])
    dt_t = _dot_nt(eye_b, dt3[0]) + _dot_nt(eye_b, dt3[1]) + _dot_nt(eye_b, dt3[2])

    eh = lax.broadcasted_iota(jnp.int32, (LANES, dinner), 0)
    el = lax.broadcasted_iota(jnp.int32, (LANES, dinner), 1)
    spread_b = _onehot_bf16(lax.shift_right_logical(el, SSM_HEAD_DIM.bit_length() - 1) == eh)
    cum_e = _dot3_r(cum3, spread_b)
    cl_e = _dot3_r(_split3(cl), spread_b)
    dt_e = _dot3_r(dt3, spread_b)
    xw_scr[...] = xs * (jnp.exp(cl_e - cum_e) * dt_e)
    cle_scr[...] = cl_e
    cm_scr[...] = cm
    bm_scr[...] = bm

    lane = lax.broadcasted_iota(jnp.int32, (rows, LANES), 1)
    heads_per_group = SSM_HEADS // SSM_GROUPS
    y_parts = []
    for g in range(SSM_GROUPS):
        gs = slice(g * SSM_STATE, (g + 1) * SSM_STATE)
        scores = _dot_nt(cm[:, gs].astype(BF16), bm[:, gs].astype(BF16))
        for j in range(g * heads_per_group // 2, (g + 1) * heads_per_group // 2):
            xp = xs[:, j * LANES:(j + 1) * LANES]
            yp = jnp.zeros((rows, LANES), F32)
            for half in range(2):
                h = 2 * j + half
                decay = jnp.where(tri, jnp.exp(cum[:, h:h + 1] - cum_t[h:h + 1, :]), 0.0)
                gm = (scores * decay * dt_t[h:h + 1, :]).astype(BF16)
                in_half = (lane < SSM_HEAD_DIM) if half == 0 else (lane >= SSM_HEAD_DIM)
                yp = yp + _dot(gm, jnp.where(in_half, xp, 0.0).astype(BF16))
            y_parts.append(yp)
    y = jnp.concatenate(y_parts, axis=1)

    if carry:
        @pl.when(pl.program_id(1) == 0)
        def _():
            h_scr[...] = jnp.zeros_like(h_scr)

    def seg_body(n, c):
        r0 = pl.multiple_of(n * seg, seg)
        for j in range(SSM_PAIRS):
            g = (2 * j) // heads_per_group
            gs = slice(g * SSM_STATE, (g + 1) * SSM_STATE)
            ps = slice(j * LANES, (j + 1) * LANES)
            hst = h_scr[j] if carry else st_in_ref[n, j]
            cm_n = cm_scr[pl.ds(r0, seg), gs].astype(BF16)
            yi_scr[pl.ds(r0, seg), ps] = _dot_nt(cm_n, hst.astype(BF16))
            xw_n = xw_scr[pl.ds(r0, seg), ps].astype(BF16)
            bm_n = bm_scr[pl.ds(r0, seg), gs].astype(BF16)
            upd = _dot_tn(xw_n, bm_n)
            h_new = _lane_column(jnp.exp(cle_scr[pl.ds(r0, 1), ps])) * hst + upd
            if carry:
                h_scr[j] = h_new
            else:
                st_out_ref[n, j] = h_new
        return c

    lax.fori_loop(0, nseg, seg_body, 0)

    y = y + jnp.exp(cum_e) * yi_scr[...] + dsk_ref[...] * xs
    z = z_ref[...]
    y = y * (z * _sigmoid(z))
    gw = dinner // SSM_GROUPS
    y = jnp.concatenate(
        [(lambda yg: yg * lax.rsqrt(jnp.mean(yg * yg, axis=-1, keepdims=True) + EPS))(
            y[:, g * gw:(g + 1) * gw]) for g in range(SSM_GROUPS)], axis=1)
    o_ref[...] = y * ng_ref[...]

    if carry:
        @pl.when(pl.program_id(1) == pl.num_programs(1) - 1)
        def _():
            st_out_ref[...] = h_scr[...]


def _dot3_r(xs, m_b):
    return _dot(xs[0], m_b) + _dot(xs[1], m_b) + _dot(xs[2], m_b)


def _attn_kernel(q_ref, k_ref, v_ref, x_ref, wo_ref, g_ref, *rest, seg):
    o_ref, att_scr = rest[-2:]
    rows = q_ref.shape[0]
    scale = 1.0 / math.sqrt(X_HEAD_DIM)

    def seg_body(n, c):
        r0 = pl.multiple_of(n * seg, seg)
        for h in range(X_HEADS):
            cs = slice(h * X_HEAD_DIM, (h + 1) * X_HEAD_DIM)
            qh = q_ref[pl.ds(r0, seg), cs].astype(BF16)
            s = _dot_nt(qh, k_ref[n, :, cs].astype(BF16)) * scale
            e = jnp.exp(s - jnp.max(s, axis=-1, keepdims=True))
            p = e / jnp.sum(e, axis=-1, keepdims=True)
            att_scr[pl.ds(r0, seg), cs] = _dot(p.astype(BF16), v_ref[n, :, cs].astype(BF16))
        return c

    lax.fori_loop(0, rows // seg, seg_body, 0)
    att = _dot(att_scr[...].astype(BF16), wo_ref[...].astype(BF16))
    o_ref[...] = x_ref[...] + _rms(att, g_ref[...])


def _params(n_grid):
    return pltpu.CompilerParams(dimension_semantics=("arbitrary",) * n_grid,
                                vmem_limit_bytes=VMEM_LIMIT)


_ANY = pl.BlockSpec(memory_space=pl.ANY)


def _vec3(p):
    return p.reshape(p.shape[0], 1, p.shape[1])


def _layer_row(n):
    return lambda layer: pl.BlockSpec((None, 1, n), lambda *_: (layer, 0, 0))


def _in_proj(x, g3, w_in, layer, *, tm, tn):
    m, d = x.shape
    n_all = w_in.shape[2]
    n_main = (n_all // LANES) * LANES
    n_tail = n_all - n_main
    return pl.pallas_call(
        functools.partial(_in_proj_kernel, n_tail=n_tail),
        grid=(m // tm, n_main // tn),
        in_specs=[
            pl.BlockSpec((tm, d), lambda i, j: (i, 0)),
            pl.BlockSpec((None, 1, d), lambda i, j: (layer, 0, 0)),
            pl.BlockSpec((None, d, tn), lambda i, j: (layer, 0, j)),
            pl.BlockSpec((None, d, LANES), lambda i, j: (layer, 0, n_main // LANES)),
        ],
        out_specs=[
            pl.BlockSpec((tm, tn), lambda i, j: (i, j)),
            pl.BlockSpec((tm, LANES), lambda i, j: (i, 0)),
        ],
        out_shape=[jax.ShapeDtypeStruct((m, n_main), F32), jax.ShapeDtypeStruct((m, LANES), F32)],
        scratch_shapes=[pltpu.VMEM((tm, d), BF16)],
        compiler_params=_params(2),
        name=f"in_proj_l{layer}",
    )(x, g3, w_in, w_in)


def _rms_matmul(x, g3, w, layer, *, tm, tn, name, stacked_prev=None, stacked=False):
    m, d = x.shape
    depth, _, n = w.shape
    in_specs = [
        pl.BlockSpec((tm, d), lambda i, j: (i, 0)),
        pl.BlockSpec((None, 1, d), lambda i, j: (layer, 0, 0)),
        pl.BlockSpec((None, d, tn), lambda i, j: (layer, 0, j)),
    ]
    args = [x, g3, w]
    aliases = {}
    if stacked:
        out_spec = pl.BlockSpec((None, tm, tn), lambda i, j: (layer, i, j))
        out_shape = jax.ShapeDtypeStruct((depth, m, n), F32)
        if stacked_prev is not None:
            in_specs.append(_ANY)
            args.append(stacked_prev)
            aliases = {3: 0}
    else:
        out_spec = pl.BlockSpec((tm, tn), lambda i, j: (i, j))
        out_shape = jax.ShapeDtypeStruct((m, n), F32)

    def body(x_ref, g_ref, w_ref, *rest):
        o_ref, xn_ref = rest[-2:]
        _rms_matmul_kernel(x_ref, g_ref, w_ref, o_ref, xn_ref)

    return pl.pallas_call(
        body,
        grid=(m // tm, n // tn),
        in_specs=in_specs,
        out_specs=out_spec,
        out_shape=out_shape,
        scratch_shapes=[pltpu.VMEM((tm, d), BF16)],
        input_output_aliases=aliases,
        compiler_params=_params(2),
        name=f"{name}_l{layer}",
    )(*args)


def _matmul_res(a, w, x, g3, layer, *, tm, tk):
    m, kdim = a.shape
    n = w.shape[2]
    return pl.pallas_call(
        _matmul_res_kernel,
        grid=(m // tm, kdim // tk),
        in_specs=[
            pl.BlockSpec((tm, tk), lambda i, k: (i, k)),
            pl.BlockSpec((None, tk, n), lambda i, k: (layer, k, 0)),
            pl.BlockSpec((tm, n), lambda i, k: (i, 0)),
            pl.BlockSpec((None, 1, n), lambda i, k: (layer, 0, 0)),
        ],
        out_specs=pl.BlockSpec((tm, n), lambda i, k: (i, 0)),
        out_shape=jax.ShapeDtypeStruct((m, n), F32),
        compiler_params=_params(2),
        name=f"out_proj_l{layer}",
    )(a, w, x, g3)


def _mlp(x, g1, w1, w2, g2, layer, *, tm, tf):
    m, d = x.shape
    dff = w1.shape[2]
    return pl.pallas_call(
        _mlp_kernel,
        grid=(m // tm, dff // tf),
        in_specs=[
            pl.BlockSpec((tm, d), lambda i, f: (i, 0)),
            pl.BlockSpec((None, 1, d), lambda i, f: (layer, 0, 0)),
            pl.BlockSpec((None, d, tf), lambda i, f: (layer, 0, f)),
            pl.BlockSpec((None, tf, d), lambda i, f: (layer, f, 0)),
            pl.BlockSpec((None, 1, d), lambda i, f: (layer, 0, 0)),
        ],
        out_specs=pl.BlockSpec((tm, d), lambda i, f: (i, 0)),
        out_shape=jax.ShapeDtypeStruct((m, d), F32),
        scratch_shapes=[pltpu.VMEM((tm, d), BF16)],
        compiler_params=_params(2),
        name=f"mlp_l{layer}",
    )(x, g1, w1, w2, g2)


def _final_rms(x, g2, row0, nrows, *, tm, name):
    d = x.shape[1]
    return pl.pallas_call(
        _final_rms_kernel,
        grid=(nrows // tm,),
        in_specs=[
            pl.BlockSpec((tm, d), lambda i: (row0 // tm + i, 0)),
            pl.BlockSpec((1, d), lambda i: (0, 0)),
        ],
        out_specs=pl.BlockSpec((tm, d), lambda i: (i, 0)),
        out_shape=jax.ShapeDtypeStruct((nrows, d), F32),
        compiler_params=_params(1),
        name=name,
    )(x, g2)


def _hgrn(proj, lb3, on3, layer, *, row0, nseq, seqlen, rows, seg, mix_shape, mix_prev, st_in, st_prev,
          st_shape, name):
    carry = st_in is None
    width = HG_HEADS * HG_DIM
    depth = lb3.shape[0]
    pad = max(seg, 8)
    if carry:
        nt = seqlen // rows
        grid = (nseq, nt)
        rmap = lambda b, t: row0 // rows + b * nt + t
        st_block = (None, None, HG_HEADS, HG_DIM, HG_DIM)
        st_map = lambda b, t: (layer, b, 0, 0, 0)
    else:
        per = rows // seg
        grid = (nseq // per, 1)
        rmap = lambda s, t: row0 // rows + s
        st_block = (None, per, HG_HEADS, HG_DIM, HG_DIM)
        st_map = lambda s, t: (layer, s, 0, 0, 0)

    col = lambda c: pl.BlockSpec((rows, width), lambda *ix: (rmap(*ix), c))
    in_specs = [col(0), col(1), col(2), col(3),
                pl.BlockSpec((depth, 1, width), lambda *ix: (0, 0, 0)),
                pl.BlockSpec((None, 1, width), lambda *ix: (layer, 0, 0))]
    args = [proj, proj, proj, proj, lb3, on3]
    if not carry:
        in_specs.append(pl.BlockSpec(st_block, st_map))
        args.append(st_in)
    n_fixed = len(args)
    aliases = {}
    if mix_prev is not None:
        aliases[len(args)] = 0
        in_specs.append(_ANY)
        args.append(mix_prev)
    if st_prev is not None:
        aliases[len(args)] = 1
        in_specs.append(_ANY)
        args.append(st_prev)

    tile = pltpu.VMEM((rows, width), F32)
    padded = pltpu.VMEM((rows + pad, width), F32)
    scratch = [tile, tile, tile, tile, padded, padded, padded]
    if carry:
        scratch = [pltpu.VMEM((HG_HEADS, HG_DIM, HG_DIM), F32)] + scratch
    return pl.pallas_call(
        functools.partial(_hgrn_kernel, layer=layer, seg=seg, carry=carry, n_alias=len(args) - n_fixed),
        grid=grid,
        in_specs=in_specs,
        out_specs=[pl.BlockSpec((rows, width), lambda *ix: (rmap(*ix), 0)),
                   pl.BlockSpec(st_block, st_map)],
        out_shape=[jax.ShapeDtypeStruct(mix_shape, F32), jax.ShapeDtypeStruct(st_shape, F32)],
        scratch_shapes=scratch,
        input_output_aliases=aliases,
        compiler_params=_params(2),
        name=f"{name}_l{layer}",
    )(*args)


def _ssd(proj, dt, cw, cb3, hp, dsk3, ng3, layer, *, row0, nseq, seqlen, rows, seg, mix_prev, st_in, cs_in,
         st_prev, cs_prev, st_shape, cs_shape, name):
    carry = st_in is None
    dinner = SSM_HEADS * SSM_HEAD_DIM
    cdim = dinner + 2 * SSM_GROUPS * SSM_STATE
    assert (4 * HG_HEADS * HG_DIM) % dinner == 0 and cdim == 2 * dinner
    z_col = (4 * HG_HEADS * HG_DIM) // dinner
    tail = CONV_W - 1
    if carry:
        nt = seqlen // rows
        grid = (nseq, nt)
        rmap = lambda b, t: row0 // rows + b * nt + t
        st_block = (None, None, SSM_PAIRS, LANES, SSM_STATE)
        cs_block = (None, None, tail, cdim)
        st_map = lambda b, t: (layer, b, 0, 0, 0)
        cs_map = lambda b, t: (layer, b, 0, 0)
    else:
        per = rows // seg
        grid = (nseq // per, 1)
        rmap = lambda s, t: row0 // rows + s
        st_block = (None, per, SSM_PAIRS, LANES, SSM_STATE)
        cs_block = (None, per, tail, cdim)
        st_map = lambda s, t: (layer, s, 0, 0, 0)
        cs_map = lambda s, t: (layer, s, 0, 0)

    in_specs = [
        pl.BlockSpec((rows, dinner), lambda *ix: (rmap(*ix), z_col)),
        pl.BlockSpec((rows, dinner), lambda *ix: (rmap(*ix), z_col + 1)),
        pl.BlockSpec((rows, dinner), lambda *ix: (rmap(*ix), z_col + 2)),
        pl.BlockSpec((rows, LANES), lambda *ix: (rmap(*ix), 0)),
        pl.BlockSpec((None, CONV_W, cdim), lambda *ix: (layer, 0, 0)),
        pl.BlockSpec((None, 1, cdim), lambda *ix: (layer, 0, 0)),
        pl.BlockSpec((None, 8, LANES), lambda *ix: (layer, 0, 0)),
        pl.BlockSpec((None, 1, dinner), lambda *ix: (layer, 0, 0)),
        pl.BlockSpec((None, 1, dinner), lambda *ix: (layer, 0, 0)),
    ]
    args = [proj, proj, proj, dt, cw, cb3, hp, dsk3, ng3]
    if not carry:
        in_specs += [pl.BlockSpec(st_block, st_map), pl.BlockSpec(cs_block, cs_map)]
        args += [st_in, cs_in]
    n_fixed = len(args)
    aliases = {len(args): 0}
    in_specs.append(_ANY)
    args.append(mix_prev)
    if st_prev is not None:
        aliases[len(args)] = 1
        aliases[len(args) + 1] = 2
        in_specs += [_ANY, _ANY]
        args += [st_prev, cs_prev]

    nseg = rows // seg
    scratch = [pltpu.VMEM((nseg, seg + 8, cdim), F32),
               pltpu.VMEM((rows, SSM_GROUPS * SSM_STATE), F32),
               pltpu.VMEM((rows, SSM_GROUPS * SSM_STATE), F32),
               pltpu.VMEM((rows, dinner), F32),
               pltpu.VMEM((rows, dinner), F32),
               pltpu.VMEM((rows, dinner), F32)]
    if carry:
        scratch = [pltpu.VMEM((SSM_PAIRS, LANES, SSM_STATE), F32)] + scratch
    mix_shape = mix_prev.shape
    return pl.pallas_call(
        functools.partial(_ssd_kernel, seg=seg, carry=carry, n_alias=len(args) - n_fixed),
        grid=grid,
        in_specs=in_specs,
        out_specs=[pl.BlockSpec((rows, dinner), lambda *ix: (rmap(*ix), HG_HEADS * HG_DIM // dinner)),
                   pl.BlockSpec(st_block, st_map),
                   pl.BlockSpec(cs_block, cs_map)],
        out_shape=[jax.ShapeDtypeStruct(mix_shape, F32), jax.ShapeDtypeStruct(st_shape, F32),
                   jax.ShapeDtypeStruct(cs_shape, F32)],
        scratch_shapes=scratch,
        input_output_aliases=aliases,
        compiler_params=_params(2),
        name=f"{name}_l{layer}",
    )(*args)


def _attn(q, mk, mv, x, wo, g3, layer, *, row0, nrows, seqlen, rows, seg, x_prev, name):
    m, d = x.shape
    xdim = q.shape[1]
    n_mem = mk.shape[2]
    per = rows // seg
    base = row0 // rows
    assert seg == min(rows, seqlen) and seqlen % seg == 0
    tiles_per_seq = seqlen // seg if per == 1 else 1
    kv_map = lambda i: (layer, i // tiles_per_seq, 0, 0)
    in_specs = [
        pl.BlockSpec((rows, xdim), lambda i: (base + i, 0)),
        pl.BlockSpec((None, per, n_mem, xdim), kv_map),
        pl.BlockSpec((None, per, n_mem, xdim), kv_map),
        pl.BlockSpec((rows, d), lambda i: (base + i, 0)),
        pl.BlockSpec((None, xdim, d), lambda i: (layer, 0, 0)),
        pl.BlockSpec((None, 1, d), lambda i: (layer, 0, 0)),
    ]
    args = [q, mk, mv, x, wo, g3]
    aliases = {}
    if x_prev is not None:
        aliases = {len(args): 0}
        in_specs.append(_ANY)
        args.append(x_prev)
    return pl.pallas_call(
        functools.partial(_attn_kernel, seg=seg),
        grid=(nrows // rows,),
        in_specs=in_specs,
        out_specs=pl.BlockSpec((rows, d), lambda i: (base + i, 0)),
        out_shape=jax.ShapeDtypeStruct((m, d), F32),
        scratch_shapes=[pltpu.VMEM((rows, xdim), F32)],
        input_output_aliases=aliases,
        compiler_params=_params(1),
        name=f"{name}_l{layer}",
    )(*args)


def kernel(x_prompt, x_sample, mem_prompt, state_hgrn, state_ssm, state_conv, cache_mem_k, cache_mem_v, g_pre_mix, g_post_mix, g_pre_x, g_post_x, g_pre_mlp, g_post_mlp, g_mem, g_final, w_in, hg_lb, hg_onorm, conv_w, conv_b, dt_bias, a_log, d_skip, ssm_norm, w_out, wq_x, wk_x, wv_x, wo_x, w_mlp1, w_mlp2):
    bp, lp, d = x_prompt.shape
    bs, ls, _ = x_sample.shape
    depth = w_in.shape[0]
    n_mem = mem_prompt.shape[1]
    mp, ms = bp * lp, bs * ls
    m = mp + ms
    xdim = X_HEADS * X_HEAD_DIM
    hg_width = HG_HEADS * HG_DIM
    dinner = SSM_HEADS * SSM_HEAD_DIM
    cdim = conv_w.shape[2]
    mix_shape = (m, hg_width + dinner)

    x = jnp.concatenate([x_prompt.reshape(mp, d), x_sample.reshape(ms, d)], axis=0)
    mem = mem_prompt.reshape(bp * n_mem, d)

    g_pre_mix3, g_post_mix3, g_pre_x3, g_post_x3 = map(_vec3, (g_pre_mix, g_post_mix, g_pre_x, g_post_x))
    g_pre_mlp3, g_post_mlp3, g_mem3 = map(_vec3, (g_pre_mlp, g_post_mlp, g_mem))
    lb3, on3, cb3, ng3 = map(_vec3, (hg_lb, hg_onorm, conv_b, ssm_norm))
    dsk3 = _vec3(jnp.repeat(d_skip, SSM_HEAD_DIM, axis=1))
    hp = jnp.zeros((depth, 8, LANES), F32)
    hp = hp.at[:, 0, :SSM_HEADS].set(dt_bias).at[:, 1, :SSM_HEADS].set(a_log)

    st_ssm_in = state_ssm.reshape(depth, bs, SSM_PAIRS, LANES, SSM_STATE)
    ck = cache_mem_k.reshape(depth, bs, n_mem, xdim)
    cv = cache_mem_v.reshape(depth, bs, n_mem, xdim)

    p_hg = p_ssm = p_conv = p_mk = p_mv = s_hg = s_ssm = s_conv = None
    p_hg_shape = (depth, bp, HG_HEADS, HG_DIM, HG_DIM)
    s_hg_shape = (depth, bs, HG_HEADS, HG_DIM, HG_DIM)
    p_ssm_shape = (depth, bp, SSM_PAIRS, LANES, SSM_STATE)
    s_ssm_shape = (depth, bs, SSM_PAIRS, LANES, SSM_STATE)
    p_conv_shape = (depth, bp, CONV_W - 1, cdim)
    s_conv_shape = (depth, bs, CONV_W - 1, cdim)

    tile = 128
    hg_seg = 16

    for layer in range(depth):
        proj, dt = _in_proj(x, g_pre_mix3, w_in, layer, tm=1024, tn=512)

        mix, p_hg = _hgrn(proj, lb3, on3, layer, row0=0, nseq=bp, seqlen=lp, rows=tile, seg=hg_seg,
                          mix_shape=mix_shape, mix_prev=None, st_in=None, st_prev=p_hg,
                          st_shape=p_hg_shape, name="hgrn_prompt")
        mix, s_hg = _hgrn(proj, lb3, on3, layer, row0=mp, nseq=bs, seqlen=ls, rows=tile, seg=ls,
                          mix_shape=mix_shape, mix_prev=mix, st_in=state_hgrn, st_prev=s_hg,
                          st_shape=s_hg_shape, name="hgrn_sample")
        mix, p_ssm, p_conv = _ssd(proj, dt, conv_w, cb3, hp, dsk3, ng3, layer, row0=0, nseq=bp, seqlen=lp,
                                  rows=tile, seg=tile, mix_prev=mix, st_in=None, cs_in=None,
                                  st_prev=p_ssm, cs_prev=p_conv, st_shape=p_ssm_shape,
                                  cs_shape=p_conv_shape, name="ssd_prompt")
        mix, s_ssm, s_conv = _ssd(proj, dt, conv_w, cb3, hp, dsk3, ng3, layer, row0=mp, nseq=bs, seqlen=ls,
                                  rows=tile, seg=ls, mix_prev=mix, st_in=st_ssm_in, cs_in=state_conv,
                                  st_prev=s_ssm, cs_prev=s_conv, st_shape=s_ssm_shape,
                                  cs_shape=s_conv_shape, name="ssd_sample")

        x = _matmul_res(mix, w_out, x, g_post_mix3, layer, tm=512, tk=512)

        q = _rms_matmul(x, g_pre_x3, wq_x, layer, tm=1024, tn=xdim, name="q_proj")
        p_mk = _rms_matmul(mem, g_mem3, wk_x, layer, tm=bp * n_mem, tn=xdim, name="mem_k",
                           stacked=True, stacked_prev=p_mk)
        p_mv = _rms_matmul(mem, g_mem3, wv_x, layer, tm=bp * n_mem, tn=xdim, name="mem_v",
                           stacked=True, stacked_prev=p_mv)
        mk4 = p_mk.reshape(depth, bp, n_mem, xdim)
        mv4 = p_mv.reshape(depth, bp, n_mem, xdim)
        x2 = _attn(q, mk4, mv4, x, wo_x, g_post_x3, layer, row0=0, nrows=mp, seqlen=lp, rows=512, seg=512,
                   x_prev=None, name="attn_prompt")
        x = _attn(q, ck, cv, x, wo_x, g_post_x3, layer, row0=mp, nrows=ms, seqlen=ls, rows=tile, seg=ls,
                  x_prev=x2, name="attn_sample")

        x = _mlp(x, g_pre_mlp3, w_mlp1, w_mlp2, g_post_mlp3, layer, tm=512, tf=512)

    g_fin = g_final.reshape(1, d)
    y_prompt = _final_rms(x, g_fin, 0, mp, tm=512, name="final_prompt").reshape(bp, lp, d)
    y_sample = _final_rms(x, g_fin, mp, ms, tm=512, name="final_sample").reshape(bs, ls, d)

    return (y_prompt, y_sample,
            p_hg,
            p_ssm.reshape(depth, bp, SSM_HEADS, SSM_HEAD_DIM, SSM_STATE),
            p_conv,
            p_mk.reshape(depth, bp, n_mem, X_HEADS, X_HEAD_DIM),
            p_mv.reshape(depth, bp, n_mem, X_HEADS, X_HEAD_DIM),
            s_hg,
            s_ssm.reshape(depth, bs, SSM_HEADS, SSM_HEAD_DIM, SSM_STATE),
            s_conv)
```

```python
import functools
import math

import jax
import jax.numpy as jnp
from jax import lax
from jax.experimental import pallas as pl
from jax.experimental.pallas import tpu as pltpu

F32 = jnp.float32
BF16 = jnp.bfloat16
EPS = 1e-6

LANES = 128
BF16_ROWS = 16
VMEM_LIMIT = 56 * 1024 * 1024

HG_HEADS = 8
HG_DIM = 128
SSM_HEADS = 16
SSM_HEAD_DIM = 64
SSM_GROUPS = 4
SSM_STATE = 128
SSM_PAIRS = SSM_HEADS // 2
CONV_W = 4
X_HEADS = 4
X_HEAD_DIM = 128


def _dot(a, b):
    return jnp.dot(a, b, preferred_element_type=F32)


def _dot_nt(a, b):
    return lax.dot_general(a, b, (((1,), (1,)), ((), ())), preferred_element_type=F32)


def _dot_tn(a, b):
    return lax.dot_general(a, b, (((0,), (0,)), ((), ())), preferred_element_type=F32)


def _split3(x):
    x1 = x.astype(BF16)
    r = x - x1.astype(F32)
    x2 = r.astype(BF16)
    r = r - x2.astype(F32)
    return x1, x2, r.astype(BF16)


def _dot3(m_b, xs):
    return _dot(m_b, xs[0]) + _dot(m_b, xs[1]) + _dot(m_b, xs[2])


def _rms(x, g):
    ms = jnp.mean(x * x, axis=-1, keepdims=True)
    return x * lax.rsqrt(ms + EPS) * g


def _sigmoid(x):
    return jax.nn.sigmoid(x)


def _log_sigmoid(x):
    return jnp.minimum(x, 0.0) - jnp.log1p(jnp.exp(-jnp.abs(x)))


def _softplus(x):
    return jnp.maximum(x, 0.0) + jnp.log1p(jnp.exp(-jnp.abs(x)))


def _onehot_bf16(mask):
    return jnp.where(mask, 1.0, 0.0).astype(BF16)


def _segment_masks(rows, seg):
    shift = seg.bit_length() - 1
    r = lax.broadcasted_iota(jnp.int32, (rows, rows), 0)
    c = lax.broadcasted_iota(jnp.int32, (rows, rows), 1)
    same = lax.shift_right_logical(r, shift) == lax.shift_right_logical(c, shift)
    return same, same & (c <= r)


def _lane_column(row):
    p1, p2, p3 = _split3(row)
    r = lax.broadcasted_iota(jnp.int32, (BF16_ROWS, LANES), 0)
    stacked = jnp.where(r == 0, p1.astype(F32),
                        jnp.where(r == 1, p2.astype(F32),
                                  jnp.where(r == 2, p3.astype(F32), 0.0))).astype(BF16)
    return _dot_tn(stacked, jnp.ones((BF16_ROWS, LANES), BF16))


def _rms_matmul_kernel(x_ref, g_ref, w_ref, o_ref, xn_ref):
    @pl.when(pl.program_id(1) == 0)
    def _():
        xn_ref[...] = _rms(x_ref[...], g_ref[...]).astype(BF16)

    o_ref[...] = _dot(xn_ref[...], w_ref[...].astype(BF16))


def _in_proj_kernel(x_ref, g_ref, w_ref, wt_ref, o_ref, ot_ref, xn_ref, *, n_tail):
    @pl.when(pl.program_id(1) == 0)
    def _():
        xn = _rms(x_ref[...], g_ref[...]).astype(BF16)
        xn_ref[...] = xn
        row = lax.broadcasted_iota(jnp.int32, wt_ref.shape, 0)
        wt = jnp.where(row < n_tail, wt_ref[...], jnp.zeros_like(wt_ref))
        ot_ref[...] = _dot_nt(xn, wt)

    o_ref[...] = _dot_nt(xn_ref[...], w_ref[...])


def _out_proj_kernel(a_ref, w_ref, x_ref, g_ref, gq_ref, wq_ref, o_ref, q_ref):
    x1 = x_ref[...] + _rms(_dot(a_ref[...].astype(BF16), w_ref[...]), g_ref[...])
    o_ref[...] = x1
    q_ref[...] = _dot(_rms(x1, gq_ref[...]).astype(BF16), wq_ref[...])


def _mlp_kernel(x_ref, g1_ref, w1_ref, w2_ref, g2_ref, o_ref, xn_ref, *, n_chunk):
    f = pl.program_id(1)

    @pl.when(f == 0)
    def _():
        xn_ref[...] = _rms(x_ref[...], g1_ref[...]).astype(BF16)
        o_ref[...] = jnp.zeros_like(o_ref)

    h = _dot(xn_ref[...], w1_ref[...].astype(BF16))
    h = jnp.square(jnp.maximum(h, 0.0)).astype(BF16)
    width = o_ref.shape[1] // n_chunk
    for c in range(n_chunk):
        cs = pl.ds(c * width, width)
        o_ref[:, cs] += _dot(h, w2_ref[:, cs].astype(BF16))

    @pl.when(f == pl.num_programs(1) - 1)
    def _():
        o_ref[...] = x_ref[...] + _rms(o_ref[...], g2_ref[...])


def _final_rms_kernel(x_ref, g_ref, o_ref):
    o_ref[...] = _rms(x_ref[...], g_ref[...])


def _lower_bound(lb_all, layer):
    depth = lb_all.shape[0]
    rows = [lb_all[i] for i in range(depth)]
    m = functools.reduce(jnp.maximum, rows)
    es = [jnp.exp(r - m) for r in rows]
    tot = functools.reduce(lambda a, b: a + b, es)
    lb = jnp.zeros_like(rows[0])
    for i in range(1, layer + 1):
        lb = lb + es[i] / tot
    return lb


def _hgrn_kernel(*refs, layer, seg, carry, n_alias):
    q_ref, f_ref, i_ref, g_ref, lb_ref, on_ref = refs[:6]
    st_in_ref = None if carry else refs[6]
    rest = refs[(6 if carry else 7) + n_alias:]
    o_ref, st_out_ref = rest[:2]
    scratch = rest[2:]
    s_scr = None
    if carry:
        s_scr, *scratch = scratch
    qt_scr, kd_scr, bt_scr, oi_scr, v_scr, ob_scr, q_scr, k_scr, b_scr, vh_scr = scratch

    rows, width = q_ref.shape
    heads = width // LANES
    nseg = rows // seg

    fr = f_ref[...]
    qr = q_ref[...]
    t = jnp.exp(-jnp.abs(fr))
    log_sig = jnp.minimum(fr, 0.0) - jnp.log1p(t)
    sig_neg = jnp.where(fr >= 0.0, t, 1.0) / (1.0 + t)
    if layer == 0:
        logf, k = log_sig, sig_neg
    else:
        lb = _lower_bound(lb_ref[...], layer)
        l1 = jnp.log(lb)
        l2 = jnp.log1p(-lb) + log_sig
        logf = jnp.maximum(l1, l2) + jnp.log1p(jnp.exp(-jnp.abs(l1 - l2)))
        k = (1.0 - lb) * sig_neg
    q = qr * _sigmoid(qr)
    v = i_ref[...]

    same, tri = _segment_masks(rows, seg)
    lf3 = _split3(logf)
    b = _dot3(_onehot_bf16(tri), lf3)
    bt = _dot3(_onehot_bf16(same), lf3)
    qt_scr[...] = q * jnp.exp(b)
    kd_scr[...] = k * jnp.exp(bt - b)
    bt_scr[...] = bt
    v_scr[...] = v

    ones_b = jnp.ones((LANES, LANES), BF16)
    for h in range(heads):
        cs = slice(h * LANES, (h + 1) * LANES)
        for ref, val in ((q_scr, q), (k_scr, k), (b_scr, b), (vh_scr, v)):
            ref[h] = val[:, cs]

        def pieces(ref):
            return [ref[h, pl.ds(tau, nseg, stride=seg), :] for tau in range(seg)]

        q_t, k_t, b_t, v_t = pieces(q_scr), pieces(k_scr), pieces(b_scr), pieces(vh_scr)
        acc = [None] * seg
        for d in range(seg):
            prod = jnp.concatenate(
                [q_t[tau] * k_t[tau - d] * jnp.exp(b_t[tau] - b_t[tau - d]) for tau in range(d, seg)],
                axis=0).astype(BF16)
            w = _dot(prod, ones_b)
            for tau in range(d, seg):
                term = w[(tau - d) * nseg:(tau - d + 1) * nseg, :] * v_t[tau - d]
                acc[tau] = term if acc[tau] is None else acc[tau] + term
        for tau in range(seg):
            ob_scr[h, pl.ds(tau, nseg, stride=seg), :] = acc[tau]

    if carry:
        @pl.when(pl.program_id(1) == 0)
        def _():
            s_scr[...] = jnp.zeros_like(s_scr)

    def seg_body(n, c):
        r0 = pl.multiple_of(n * seg, seg)
        for h in range(heads):
            cs = slice(h * LANES, (h + 1) * LANES)
            s = s_scr[h] if carry else st_in_ref[n, h]
            qt_n = qt_scr[pl.ds(r0, seg), cs].astype(BF16)
            oi_scr[pl.ds(r0, seg), cs] = _dot(qt_n, s.astype(BF16))
            kd_n = kd_scr[pl.ds(r0, seg), cs].astype(BF16)
            v_n = v_scr[pl.ds(r0, seg), cs].astype(BF16)
            upd = _dot_tn(kd_n, v_n)
            s_new = _lane_column(jnp.exp(bt_scr[pl.ds(r0, 1), cs])) * s + upd
            if carry:
                s_scr[h] = s_new
            else:
                st_out_ref[n, h] = s_new
        return c

    lax.fori_loop(0, rows // seg, seg_body, 0)

    o = jnp.concatenate(
        [(lambda oh: oh * lax.rsqrt(jnp.mean(oh * oh, axis=-1, keepdims=True) + EPS))(
            ob_scr[h] + oi_scr[:, h * LANES:(h + 1) * LANES]) for h in range(heads)], axis=1)
    o_ref[...] = o * on_ref[...] * _sigmoid(g_ref[...])

    if carry:
        @pl.when(pl.program_id(1) == pl.num_programs(1) - 1)
        def _():
            st_out_ref[...] = s_scr[...]


def _ssd_kernel(*refs, seg, carry, n_alias):
    z_ref, xa_ref, xb_ref, dt_ref, cw_ref, cb_ref, hp_ref, dsk_ref, ng_ref = refs[:9]
    st_in_ref, cs_in_ref = (None, None) if carry else refs[9:11]
    rest = refs[(9 if carry else 11) + n_alias:]
    o_ref, st_out_ref, cs_out_ref = rest[:3]
    scratch = rest[3:]
    h_scr = None
    if carry:
        h_scr, *scratch = scratch
    xpad, cm_scr, bm_scr, xw_scr, cle_scr, yi_scr = scratch

    rows, dinner = z_ref.shape
    cdim = xa_ref.shape[1] + xb_ref.shape[1]
    nseg = rows // seg
    gn = (cdim - dinner) // 2
    head_pad = xpad.shape[1] - seg
    tail = CONV_W - 1

    if carry:
        @pl.when(pl.program_id(1) == 0)
        def _():
            xpad[:, pl.ds(head_pad - tail, tail), :] = jnp.zeros((nseg, tail, cdim), F32)
    else:
        xpad[:, pl.ds(head_pad - tail, tail), :] = cs_in_ref[...]
    xpad[:, pl.ds(head_pad, seg), pl.ds(0, dinner)] = xa_ref[...].reshape(nseg, seg, dinner)
    xpad[:, pl.ds(head_pad, seg), pl.ds(dinner, cdim - dinner)] = xb_ref[...].reshape(nseg, seg, cdim - dinner)
    cw = cw_ref[...]
    conv = cb_ref[...] + xpad[:, pl.ds(head_pad - tail, seg), :] * cw[0:1, :]
    for j in range(1, CONV_W):
        conv = conv + xpad[:, pl.ds(head_pad - tail + j, seg), :] * cw[j:j + 1, :]
    new_tail = xpad[:, pl.ds(head_pad + seg - tail, tail), :]
    cs_out_ref[...] = new_tail.reshape(cs_out_ref.shape)
    if carry:
        xpad[:, pl.ds(head_pad - tail, tail), :] = new_tail
    u = conv.reshape(rows, cdim)
    u = u * _sigmoid(u)
    xs = u[:, :dinner]
    bm = u[:, dinner:dinner + gn]
    cm = u[:, dinner + gn:]

    hp = hp_ref[...]
    dt = _softplus(dt_ref[...] + hp[0:1, :])
    a = dt * (-jnp.exp(hp[1:2, :]))
    same, tri = _segment_masks(rows, seg)
    a3 = _split3(a)
    cum = _dot3(_onehot_bf16(tri), a3)
    cl = _dot3(_onehot_bf16(same), a3)
    cum3 = _split3(cum)
    dt3 = _split3(dt)
    hh = lax.broadcasted_iota(jnp.int32, (BF16_ROWS, LANES), 0)
    hl = lax.broadcasted_iota(jnp.int32, (BF16_ROWS, LANES), 1)
    eye_b = _onehot_bf16(hh == hl)
    cum_t = _dot_nt(eye_b, cum3[0]) + _dot_nt(eye_b, cum3[1]) + _dot_nt(eye_b, cum3[2])
    dt_t = _dot_nt(eye_b, dt3[0]) + _dot_nt(eye_b, dt3[1]) + _dot_nt(eye_b, dt3[2])

    eh = lax.broadcasted_iota(jnp.int32, (LANES, dinner), 0)
    el = lax.broadcasted_iota(jnp.int32, (LANES, dinner), 1)
    spread_b = _onehot_bf16(lax.shift_right_logical(el, SSM_HEAD_DIM.bit_length() - 1) == eh)
    cum_e = _dot3_r(cum3, spread_b)
    cl_e = _dot3_r(_split3(cl), spread_b)
    dt_e = _dot3_r(dt3, spread_b)
    xw_scr[...] = xs * (jnp.exp(cl_e - cum_e) * dt_e)
    cle_scr[...] = cl_e
    cm_scr[...] = cm
    bm_scr[...] = bm

    lane = lax.broadcasted_iota(jnp.int32, (rows, LANES), 1)
    heads_per_group = SSM_HEADS // SSM_GROUPS
    y_parts = []
    for g in range(SSM_GROUPS):
        gs = slice(g * SSM_STATE, (g + 1) * SSM_STATE)
        scores = _dot_nt(cm[:, gs].astype(BF16), bm[:, gs].astype(BF16))
        for j in range(g * heads_per_group // 2, (g + 1) * heads_per_group // 2):
            xp = xs[:, j * LANES:(j + 1) * LANES]
            yp = jnp.zeros((rows, LANES), F32)
            for half in range(2):
                h = 2 * j + half
                decay = jnp.where(tri, jnp.exp(cum[:, h:h + 1] - cum_t[h:h + 1, :]), 0.0)
                gm = (scores * decay * dt_t[h:h + 1, :]).astype(BF16)
                in_half = (lane < SSM_HEAD_DIM) if half == 0 else (lane >= SSM_HEAD_DIM)
                yp = yp + _dot(gm, jnp.where(in_half, xp, 0.0).astype(BF16))
            y_parts.append(yp)
    y = jnp.concatenate(y_parts, axis=1)

    if carry:
        @pl.when(pl.program_id(1) == 0)
        def _():
            h_scr[...] = jnp.zeros_like(h_scr)

    def seg_body(n, c):
        r0 = pl.multiple_of(n * seg, seg)
        for j in range(SSM_PAIRS):
            g = (2 * j) // heads_per_group
            gs = slice(g * SSM_STATE, (g + 1) * SSM_STATE)
            ps = slice(j * LANES, (j + 1) * LANES)
            hst = h_scr[j] if carry else st_in_ref[n, j]
            cm_n = cm_scr[pl.ds(r0, seg), gs].astype(BF16)
            yi_scr[pl.ds(r0, seg), ps] = _dot_nt(cm_n, hst.astype(BF16))
            xw_n = xw_scr[pl.ds(r0, seg), ps].astype(BF16)
            bm_n = bm_scr[pl.ds(r0, seg), gs].astype(BF16)
            upd = _dot_tn(xw_n, bm_n)
            h_new = _lane_column(jnp.exp(cle_scr[pl.ds(r0, 1), ps])) * hst + upd
            if carry:
                h_scr[j] = h_new
            else:
                st_out_ref[n, j] = h_new
        return c

    lax.fori_loop(0, nseg, seg_body, 0)

    y = y + jnp.exp(cum_e) * yi_scr[...] + dsk_ref[...] * xs
    z = z_ref[...]
    y = y * (z * _sigmoid(z))
    gw = dinner // SSM_GROUPS
    y = jnp.concatenate(
        [(lambda yg: yg * lax.rsqrt(jnp.mean(yg * yg, axis=-1, keepdims=True) + EPS))(
            y[:, g * gw:(g + 1) * gw]) for g in range(SSM_GROUPS)], axis=1)
    o_ref[...] = y * ng_ref[...]

    if carry:
        @pl.when(pl.program_id(1) == pl.num_programs(1) - 1)
        def _():
            st_out_ref[...] = h_scr[...]


def _dot3_r(xs, m_b):
    return _dot(xs[0], m_b) + _dot(xs[1], m_b) + _dot(xs[2], m_b)


def _attn_kernel(q_ref, k_ref, v_ref, x_ref, wo_ref, g_ref, *rest, seg):
    o_ref, att_scr = rest[-2:]
    rows = q_ref.shape[0]
    scale = 1.0 / math.sqrt(X_HEAD_DIM)

    def seg_body(n, c):
        r0 = pl.multiple_of(n * seg, seg)
        for h in range(X_HEADS):
            cs = slice(h * X_HEAD_DIM, (h + 1) * X_HEAD_DIM)
            qh = q_ref[pl.ds(r0, seg), cs].astype(BF16)
            s = _dot_nt(qh, k_ref[n, :, cs].astype(BF16)) * scale
            e = jnp.exp(s - jnp.max(s, axis=-1, keepdims=True))
            p = e / jnp.sum(e, axis=-1, keepdims=True)
            att_scr[pl.ds(r0, seg), cs] = _dot(p.astype(BF16), v_ref[n, :, cs].astype(BF16))
        return c

    lax.fori_loop(0, rows // seg, seg_body, 0)
    att = _dot(att_scr[...].astype(BF16), wo_ref[...])
    o_ref[...] = x_ref[...] + _rms(att, g_ref[...])


def _attn_interleaved_kernel(q_ref, k_ref, v_ref, x_ref, wo_ref, g_ref, *rest, seg):
    o_ref, att_scr = rest[-2:]
    rows = q_ref.shape[0]
    nkv = k_ref.shape[1]
    scale = 1.0 / math.sqrt(X_HEAD_DIM)
    rr = lax.broadcasted_iota(jnp.int32, (X_HEADS * seg, nkv), 0)
    cc = lax.broadcasted_iota(jnp.int32, (X_HEADS * seg, nkv), 1)
    own_head = (cc & (X_HEADS - 1)) == lax.shift_right_logical(rr, seg.bit_length() - 1)

    def seg_body(n, c):
        r0 = pl.multiple_of(n * seg, seg)
        qs = jnp.concatenate(
            [q_ref[pl.ds(r0, seg), h * X_HEAD_DIM:(h + 1) * X_HEAD_DIM] for h in range(X_HEADS)],
            axis=0).astype(BF16)
        s = jnp.where(own_head, _dot_nt(qs, k_ref[n].astype(BF16)) * scale, -jnp.inf)
        e = jnp.exp(s - jnp.max(s, axis=-1, keepdims=True))
        p = e / jnp.sum(e, axis=-1, keepdims=True)
        o = _dot(p.astype(BF16), v_ref[n].astype(BF16))
        for h in range(X_HEADS):
            att_scr[pl.ds(r0, seg), h * X_HEAD_DIM:(h + 1) * X_HEAD_DIM] = o[h * seg:(h + 1) * seg]
        return c

    lax.fori_loop(0, rows // seg, seg_body, 0)
    att = _dot(att_scr[...].astype(BF16), wo_ref[...])
    o_ref[...] = x_ref[...] + _rms(att, g_ref[...])


def _params(n_grid):
    return pltpu.CompilerParams(dimension_semantics=("arbitrary",) * n_grid,
                                vmem_limit_bytes=VMEM_LIMIT)


_ANY = pl.BlockSpec(memory_space=pl.ANY)


def _vec3(p):
    return p.reshape(p.shape[0], 1, p.shape[1])


def _layer_row(n):
    return lambda layer: pl.BlockSpec((None, 1, n), lambda *_: (layer, 0, 0))


def _in_proj(x, g3, w_in_t, layer, *, tm, tn):
    m, d = x.shape
    n_all = w_in_t.shape[1]
    n_main = (n_all // LANES) * LANES
    n_tail = n_all - n_main
    return pl.pallas_call(
        functools.partial(_in_proj_kernel, n_tail=n_tail),
        grid=(m // tm, n_main // tn),
        in_specs=[
            pl.BlockSpec((tm, d), lambda i, j: (i, 0)),
            pl.BlockSpec((None, 1, d), lambda i, j: (layer, 0, 0)),
            pl.BlockSpec((None, tn, d), lambda i, j: (layer, j, 0)),
            pl.BlockSpec((None, LANES, d), lambda i, j: (layer, n_main // LANES, 0)),
        ],
        out_specs=[
            pl.BlockSpec((tm, tn), lambda i, j: (i, j)),
            pl.BlockSpec((tm, LANES), lambda i, j: (i, 0)),
        ],
        out_shape=[jax.ShapeDtypeStruct((m, n_main), F32), jax.ShapeDtypeStruct((m, LANES), F32)],
        scratch_shapes=[pltpu.VMEM((tm, d), BF16)],
        compiler_params=_params(2),
        name=f"in_proj_l{layer}",
    )(x, g3, w_in_t, w_in_t)


def _rms_matmul(x, g3, w, layer, *, tm, tn, name, stacked_prev=None, stacked=False):
    m, d = x.shape
    depth, _, n = w.shape
    in_specs = [
        pl.BlockSpec((tm, d), lambda i, j: (i, 0)),
        pl.BlockSpec((None, 1, d), lambda i, j: (layer, 0, 0)),
        pl.BlockSpec((None, d, tn), lambda i, j: (layer, 0, j)),
    ]
    args = [x, g3, w]
    aliases = {}
    if stacked:
        out_spec = pl.BlockSpec((None, tm, tn), lambda i, j: (layer, i, j))
        out_shape = jax.ShapeDtypeStruct((depth, m, n), F32)
        if stacked_prev is not None:
            in_specs.append(_ANY)
            args.append(stacked_prev)
            aliases = {3: 0}
    else:
        out_spec = pl.BlockSpec((tm, tn), lambda i, j: (i, j))
        out_shape = jax.ShapeDtypeStruct((m, n), F32)

    def body(x_ref, g_ref, w_ref, *rest):
        o_ref, xn_ref = rest[-2:]
        _rms_matmul_kernel(x_ref, g_ref, w_ref, o_ref, xn_ref)

    return pl.pallas_call(
        body,
        grid=(m // tm, n // tn),
        in_specs=in_specs,
        out_specs=out_spec,
        out_shape=out_shape,
        scratch_shapes=[pltpu.VMEM((tm, d), BF16)],
        input_output_aliases=aliases,
        compiler_params=_params(2),
        name=f"{name}_l{layer}",
    )(*args)


def _out_proj(a, w_b, x, g3, gq3, wq_b, layer, *, tm):
    m, kdim = a.shape
    n = w_b.shape[2]
    nq = wq_b.shape[2]
    return pl.pallas_call(
        _out_proj_kernel,
        grid=(m // tm,),
        in_specs=[
            pl.BlockSpec((tm, kdim), lambda i: (i, 0)),
            pl.BlockSpec((None, kdim, n), lambda i: (layer, 0, 0)),
            pl.BlockSpec((tm, n), lambda i: (i, 0)),
            pl.BlockSpec((None, 1, n), lambda i: (layer, 0, 0)),
            pl.BlockSpec((None, 1, n), lambda i: (layer, 0, 0)),
            pl.BlockSpec((None, n, nq), lambda i: (layer, 0, 0)),
        ],
        out_specs=[pl.BlockSpec((tm, n), lambda i: (i, 0)),
                   pl.BlockSpec((tm, nq), lambda i: (i, 0))],
        out_shape=[jax.ShapeDtypeStruct((m, n), F32), jax.ShapeDtypeStruct((m, nq), F32)],
        compiler_params=_params(1),
        name=f"out_proj_l{layer}",
    )(a, w_b, x, g3, gq3, wq_b)


def _mlp(x, g1, w1, w2, g2, layer, *, tm, tf):
    m, d = x.shape
    dff = w1.shape[2]
    single = pl.Buffered(1)
    return pl.pallas_call(
        functools.partial(_mlp_kernel, n_chunk=4),
        grid=(m // tm, dff // tf),
        in_specs=[
            pl.BlockSpec((tm, d), lambda i, f: (i, 0), pipeline_mode=single),
            pl.BlockSpec((None, 1, d), lambda i, f: (layer, 0, 0)),
            pl.BlockSpec((None, d, tf), lambda i, f: (layer, 0, f)),
            pl.BlockSpec((None, tf, d), lambda i, f: (layer, f, 0)),
            pl.BlockSpec((None, 1, d), lambda i, f: (layer, 0, 0)),
        ],
        out_specs=pl.BlockSpec((tm, d), lambda i, f: (i, 0)),
        out_shape=jax.ShapeDtypeStruct((m, d), F32),
        scratch_shapes=[pltpu.VMEM((tm, d), BF16)],
        compiler_params=_params(2),
        name=f"mlp_l{layer}",
    )(x, g1, w1, w2, g2)


def _final_rms(x, g2, row0, nrows, *, tm, name):
    d = x.shape[1]
    return pl.pallas_call(
        _final_rms_kernel,
        grid=(nrows // tm,),
        in_specs=[
            pl.BlockSpec((tm, d), lambda i: (row0 // tm + i, 0)),
            pl.BlockSpec((1, d), lambda i: (0, 0)),
        ],
        out_specs=pl.BlockSpec((tm, d), lambda i: (i, 0)),
        out_shape=jax.ShapeDtypeStruct((nrows, d), F32),
        compiler_params=_params(1),
        name=name,
    )(x, g2)


def _hgrn(proj, lb3, on3, layer, *, row0, nseq, seqlen, rows, seg, mix_shape, mix_prev, st_in, st_prev,
          st_shape, name):
    carry = st_in is None
    width = HG_HEADS * HG_DIM
    depth = lb3.shape[0]
    if carry:
        nt = seqlen // rows
        grid = (nseq, nt)
        rmap = lambda b, t: row0 // rows + b * nt + t
        st_block = (None, None, HG_HEADS, HG_DIM, HG_DIM)
        st_map = lambda b, t: (layer, b, 0, 0, 0)
    else:
        per = rows // seg
        grid = (nseq // per, 1)
        rmap = lambda s, t: row0 // rows + s
        st_block = (None, per, HG_HEADS, HG_DIM, HG_DIM)
        st_map = lambda s, t: (layer, s, 0, 0, 0)

    col = lambda c: pl.BlockSpec((rows, width), lambda *ix: (rmap(*ix), c))
    in_specs = [col(0), col(1), col(2), col(3),
                pl.BlockSpec((depth, 1, width), lambda *ix: (0, 0, 0)),
                pl.BlockSpec((None, 1, width), lambda *ix: (layer, 0, 0))]
    args = [proj, proj, proj, proj, lb3, on3]
    if not carry:
        in_specs.append(pl.BlockSpec(st_block, st_map))
        args.append(st_in)
    n_fixed = len(args)
    aliases = {}
    if mix_prev is not None:
        aliases[len(args)] = 0
        in_specs.append(_ANY)
        args.append(mix_prev)
    if st_prev is not None:
        aliases[len(args)] = 1
        in_specs.append(_ANY)
        args.append(st_prev)

    tile = pltpu.VMEM((rows, width), F32)
    per_head = pltpu.VMEM((HG_HEADS, rows, HG_DIM), F32)
    scratch = [tile] * 5 + [per_head] * 5
    if carry:
        scratch = [pltpu.VMEM((HG_HEADS, HG_DIM, HG_DIM), F32)] + scratch
    return pl.pallas_call(
        functools.partial(_hgrn_kernel, layer=layer, seg=seg, carry=carry, n_alias=len(args) - n_fixed),
        grid=grid,
        in_specs=in_specs,
        out_specs=[pl.BlockSpec((rows, width), lambda *ix: (rmap(*ix), 0)),
                   pl.BlockSpec(st_block, st_map)],
        out_shape=[jax.ShapeDtypeStruct(mix_shape, F32), jax.ShapeDtypeStruct(st_shape, F32)],
        scratch_shapes=scratch,
        input_output_aliases=aliases,
        compiler_params=_params(2),
        name=f"{name}_l{layer}",
    )(*args)


def _ssd(proj, dt, cw, cb3, hp, dsk3, ng3, layer, *, row0, nseq, seqlen, rows, seg, mix_prev, st_in, cs_in,
         st_prev, cs_prev, st_shape, cs_shape, name):
    carry = st_in is None
    dinner = SSM_HEADS * SSM_HEAD_DIM
    cdim = dinner + 2 * SSM_GROUPS * SSM_STATE
    assert (4 * HG_HEADS * HG_DIM) % dinner == 0 and cdim == 2 * dinner
    z_col = (4 * HG_HEADS * HG_DIM) // dinner
    tail = CONV_W - 1
    if carry:
        nt = seqlen // rows
        grid = (nseq, nt)
        rmap = lambda b, t: row0 // rows + b * nt + t
        st_block = (None, None, SSM_PAIRS, LANES, SSM_STATE)
        cs_block = (None, None, tail, cdim)
        st_map = lambda b, t: (layer, b, 0, 0, 0)
        cs_map = lambda b, t: (layer, b, 0, 0)
    else:
        per = rows // seg
        grid = (nseq // per, 1)
        rmap = lambda s, t: row0 // rows + s
        st_block = (None, per, SSM_PAIRS, LANES, SSM_STATE)
        cs_block = (None, per, tail, cdim)
        st_map = lambda s, t: (layer, s, 0, 0, 0)
        cs_map = lambda s, t: (layer, s, 0, 0)

    in_specs = [
        pl.BlockSpec((rows, dinner), lambda *ix: (rmap(*ix), z_col)),
        pl.BlockSpec((rows, dinner), lambda *ix: (rmap(*ix), z_col + 1)),
        pl.BlockSpec((rows, dinner), lambda *ix: (rmap(*ix), z_col + 2)),
        pl.BlockSpec((rows, LANES), lambda *ix: (rmap(*ix), 0)),
        pl.BlockSpec((None, CONV_W, cdim), lambda *ix: (layer, 0, 0)),
        pl.BlockSpec((None, 1, cdim), lambda *ix: (layer, 0, 0)),
        pl.BlockSpec((None, 8, LANES), lambda *ix: (layer, 0, 0)),
        pl.BlockSpec((None, 1, dinner), lambda *ix: (layer, 0, 0)),
        pl.BlockSpec((None, 1, dinner), lambda *ix: (layer, 0, 0)),
    ]
    args = [proj, proj, proj, dt, cw, cb3, hp, dsk3, ng3]
    if not carry:
        in_specs += [pl.BlockSpec(st_block, st_map), pl.BlockSpec(cs_block, cs_map)]
        args += [st_in, cs_in]
    n_fixed = len(args)
    aliases = {len(args): 0}
    in_specs.append(_ANY)
    args.append(mix_prev)
    if st_prev is not None:
        aliases[len(args)] = 1
        aliases[len(args) + 1] = 2
        in_specs += [_ANY, _ANY]
        args += [st_prev, cs_prev]

    nseg = rows // seg
    scratch = [pltpu.VMEM((nseg, seg + 8, cdim), F32),
               pltpu.VMEM((rows, SSM_GROUPS * SSM_STATE), F32),
               pltpu.VMEM((rows, SSM_GROUPS * SSM_STATE), F32),
               pltpu.VMEM((rows, dinner), F32),
               pltpu.VMEM((rows, dinner), F32),
               pltpu.VMEM((rows, dinner), F32)]
    if carry:
        scratch = [pltpu.VMEM((SSM_PAIRS, LANES, SSM_STATE), F32)] + scratch
    mix_shape = mix_prev.shape
    return pl.pallas_call(
        functools.partial(_ssd_kernel, seg=seg, carry=carry, n_alias=len(args) - n_fixed),
        grid=grid,
        in_specs=in_specs,
        out_specs=[pl.BlockSpec((rows, dinner), lambda *ix: (rmap(*ix), HG_HEADS * HG_DIM // dinner)),
                   pl.BlockSpec(st_block, st_map),
                   pl.BlockSpec(cs_block, cs_map)],
        out_shape=[jax.ShapeDtypeStruct(mix_shape, F32), jax.ShapeDtypeStruct(st_shape, F32),
                   jax.ShapeDtypeStruct(cs_shape, F32)],
        scratch_shapes=scratch,
        input_output_aliases=aliases,
        compiler_params=_params(2),
        name=f"{name}_l{layer}",
    )(*args)


def _attn(q, mk, mv, x, wo, g3, layer, *, row0, nrows, seqlen, rows, seg, x_prev, name):
    m, d = x.shape
    xdim = q.shape[1]
    interleaved = mk.shape[3] == X_HEAD_DIM
    body = _attn_interleaved_kernel if interleaved else _attn_kernel
    per = rows // seg
    base = row0 // rows
    assert seg == min(rows, seqlen) and seqlen % seg == 0
    tiles_per_seq = seqlen // seg if per == 1 else 1
    kv_map = lambda i: (layer, i // tiles_per_seq, 0, 0)
    in_specs = [
        pl.BlockSpec((rows, xdim), lambda i: (base + i, 0)),
        pl.BlockSpec((None, per) + mk.shape[2:], kv_map),
        pl.BlockSpec((None, per) + mv.shape[2:], kv_map),
        pl.BlockSpec((rows, d), lambda i: (base + i, 0)),
        pl.BlockSpec((None, xdim, d), lambda i: (layer, 0, 0)),
        pl.BlockSpec((None, 1, d), lambda i: (layer, 0, 0)),
    ]
    args = [q, mk, mv, x, wo, g3]
    aliases = {}
    if x_prev is not None:
        aliases = {len(args): 0}
        in_specs.append(_ANY)
        args.append(x_prev)
    return pl.pallas_call(
        functools.partial(body, seg=seg),
        grid=(nrows // rows,),
        in_specs=in_specs,
        out_specs=pl.BlockSpec((rows, d), lambda i: (base + i, 0)),
        out_shape=jax.ShapeDtypeStruct((m, d), F32),
        scratch_shapes=[pltpu.VMEM((rows, xdim), F32)],
        input_output_aliases=aliases,
        compiler_params=_params(1),
        name=f"{name}_l{layer}",
    )(*args)


def kernel(x_prompt, x_sample, mem_prompt, state_hgrn, state_ssm, state_conv, cache_mem_k, cache_mem_v, g_pre_mix, g_post_mix, g_pre_x, g_post_x, g_pre_mlp, g_post_mlp, g_mem, g_final, w_in, hg_lb, hg_onorm, conv_w, conv_b, dt_bias, a_log, d_skip, ssm_norm, w_out, wq_x, wk_x, wv_x, wo_x, w_mlp1, w_mlp2):
    bp, lp, d = x_prompt.shape
    bs, ls, _ = x_sample.shape
    depth = w_in.shape[0]
    n_mem = mem_prompt.shape[1]
    mp, ms = bp * lp, bs * ls
    m = mp + ms
    xdim = X_HEADS * X_HEAD_DIM
    hg_width = HG_HEADS * HG_DIM
    dinner = SSM_HEADS * SSM_HEAD_DIM
    cdim = conv_w.shape[2]
    mix_shape = (m, hg_width + dinner)

    x = jnp.concatenate([x_prompt.reshape(mp, d), x_sample.reshape(ms, d)], axis=0)
    mem = mem_prompt.reshape(bp * n_mem, d)

    g_pre_mix3, g_post_mix3, g_pre_x3, g_post_x3 = map(_vec3, (g_pre_mix, g_post_mix, g_pre_x, g_post_x))
    g_pre_mlp3, g_post_mlp3, g_mem3 = map(_vec3, (g_pre_mlp, g_post_mlp, g_mem))
    lb3, on3, cb3, ng3 = map(_vec3, (hg_lb, hg_onorm, conv_b, ssm_norm))
    dsk3 = _vec3(jnp.repeat(d_skip, SSM_HEAD_DIM, axis=1))
    hp = jnp.zeros((depth, 8, LANES), F32)
    hp = hp.at[:, 0, :SSM_HEADS].set(dt_bias).at[:, 1, :SSM_HEADS].set(a_log)

    st_ssm_in = state_ssm.reshape(depth, bs, SSM_PAIRS, LANES, SSM_STATE)
    ck = cache_mem_k.reshape(depth, bs, n_mem * X_HEADS, X_HEAD_DIM)
    cv = cache_mem_v.reshape(depth, bs, n_mem * X_HEADS, X_HEAD_DIM)

    w_in_t = jnp.swapaxes(w_in, 1, 2).astype(BF16)
    w_out_b, wq_b, wo_b = w_out.astype(BF16), wq_x.astype(BF16), wo_x.astype(BF16)

    p_hg = p_ssm = p_conv = p_mk = p_mv = s_hg = s_ssm = s_conv = None
    p_hg_shape = (depth, bp, HG_HEADS, HG_DIM, HG_DIM)
    s_hg_shape = (depth, bs, HG_HEADS, HG_DIM, HG_DIM)
    p_ssm_shape = (depth, bp, SSM_PAIRS, LANES, SSM_STATE)
    s_ssm_shape = (depth, bs, SSM_PAIRS, LANES, SSM_STATE)
    p_conv_shape = (depth, bp, CONV_W - 1, cdim)
    s_conv_shape = (depth, bs, CONV_W - 1, cdim)

    tile = 128
    hg_seg = 16

    for layer in range(depth):
        proj, dt = _in_proj(x, g_pre_mix3, w_in_t, layer, tm=1024, tn=512)

        mix, p_hg = _hgrn(proj, lb3, on3, layer, row0=0, nseq=bp, seqlen=lp, rows=tile, seg=hg_seg,
                          mix_shape=mix_shape, mix_prev=None, st_in=None, st_prev=p_hg,
                          st_shape=p_hg_shape, name="hgrn_prompt")
        mix, s_hg = _hgrn(proj, lb3, on3, layer, row0=mp, nseq=bs, seqlen=ls, rows=tile, seg=ls,
                          mix_shape=mix_shape, mix_prev=mix, st_in=state_hgrn, st_prev=s_hg,
                          st_shape=s_hg_shape, name="hgrn_sample")
        mix, p_ssm, p_conv = _ssd(proj, dt, conv_w, cb3, hp, dsk3, ng3, layer, row0=0, nseq=bp, seqlen=lp,
                                  rows=tile, seg=tile, mix_prev=mix, st_in=None, cs_in=None,
                                  st_prev=p_ssm, cs_prev=p_conv, st_shape=p_ssm_shape,
                                  cs_shape=p_conv_shape, name="ssd_prompt")
        mix, s_ssm, s_conv = _ssd(proj, dt, conv_w, cb3, hp, dsk3, ng3, layer, row0=mp, nseq=bs, seqlen=ls,
                                  rows=tile, seg=ls, mix_prev=mix, st_in=st_ssm_in, cs_in=state_conv,
                                  st_prev=s_ssm, cs_prev=s_conv, st_shape=s_ssm_shape,
                                  cs_shape=s_conv_shape, name="ssd_sample")

        x, q = _out_proj(mix, w_out_b, x, g_post_mix3, g_pre_x3, wq_b, layer, tm=512)
        p_mk = _rms_matmul(mem, g_mem3, wk_x, layer, tm=bp * n_mem, tn=xdim, name="mem_k",
                           stacked=True, stacked_prev=p_mk)
        p_mv = _rms_matmul(mem, g_mem3, wv_x, layer, tm=bp * n_mem, tn=xdim, name="mem_v",
                           stacked=True, stacked_prev=p_mv)
        mk4 = p_mk.reshape(depth, bp, n_mem, xdim)
        mv4 = p_mv.reshape(depth, bp, n_mem, xdim)
        x2 = _attn(q, mk4, mv4, x, wo_b, g_post_x3, layer, row0=0, nrows=mp, seqlen=lp, rows=512, seg=512,
                   x_prev=None, name="attn_prompt")
        x = _attn(q, ck, cv, x, wo_b, g_post_x3, layer, row0=mp, nrows=ms, seqlen=ls, rows=tile, seg=ls,
                  x_prev=x2, name="attn_sample")

        x = _mlp(x, g_pre_mlp3, w_mlp1, w_mlp2, g_post_mlp3, layer, tm=1024, tf=512)

    g_fin = g_final.reshape(1, d)
    y_prompt = _final_rms(x, g_fin, 0, mp, tm=512, name="final_prompt").reshape(bp, lp, d)
    y_sample = _final_rms(x, g_fin, mp, ms, tm=512, name="final_sample").reshape(bs, ls, d)

    return (y_prompt, y_sample,
            p_hg,
            p_ssm.reshape(depth, bp, SSM_HEADS, SSM_HEAD_DIM, SSM_STATE),
            p_conv,
            p_mk.reshape(depth, bp, n_mem, X_HEADS, X_HEAD_DIM),
            p_mv.reshape(depth, bp, n_mem, X_HEADS, X_HEAD_DIM),
            s_hg,
            s_ssm.reshape(depth, bs, SSM_HEADS, SSM_HEAD_DIM, SSM_STATE),
            s_conv)
```

```python
import functools
import math

import jax
import jax.numpy as jnp
from jax import lax
from jax.experimental import pallas as pl
from jax.experimental.pallas import tpu as pltpu

F32 = jnp.float32
BF16 = jnp.bfloat16
EPS = 1e-6

LANES = 128
BF16_ROWS = 16
VMEM_LIMIT = 56 * 1024 * 1024

HG_HEADS = 8
HG_DIM = 128
SSM_HEADS = 16
SSM_HEAD_DIM = 64
SSM_GROUPS = 4
SSM_STATE = 128
SSM_PAIRS = SSM_HEADS // 2
CONV_W = 4
X_HEADS = 4
X_HEAD_DIM = 128


def _dot(a, b):
    return jnp.dot(a, b, preferred_element_type=F32)


def _dot_nt(a, b):
    return lax.dot_general(a, b, (((1,), (1,)), ((), ())), preferred_element_type=F32)


def _dot_tn(a, b):
    return lax.dot_general(a, b, (((0,), (0,)), ((), ())), preferred_element_type=F32)


def _split3(x):
    x1 = x.astype(BF16)
    r = x - x1.astype(F32)
    x2 = r.astype(BF16)
    r = r - x2.astype(F32)
    return x1, x2, r.astype(BF16)


def _dot3(m_b, xs):
    return _dot(m_b, xs[0]) + _dot(m_b, xs[1]) + _dot(m_b, xs[2])


def _rms(x, g):
    ms = jnp.mean(x * x, axis=-1, keepdims=True)
    return x * lax.rsqrt(ms + EPS) * g


def _sigmoid(x):
    return jax.nn.sigmoid(x)


def _log_sigmoid(x):
    return jnp.minimum(x, 0.0) - jnp.log1p(jnp.exp(-jnp.abs(x)))


def _softplus(x):
    return jnp.maximum(x, 0.0) + jnp.log1p(jnp.exp(-jnp.abs(x)))


def _onehot_bf16(mask):
    return jnp.where(mask, 1.0, 0.0).astype(BF16)


def _segment_masks(rows, seg):
    shift = seg.bit_length() - 1
    r = lax.broadcasted_iota(jnp.int32, (rows, rows), 0)
    c = lax.broadcasted_iota(jnp.int32, (rows, rows), 1)
    same = lax.shift_right_logical(r, shift) == lax.shift_right_logical(c, shift)
    return same, same & (c <= r)


def _lane_column(row):
    p1, p2, p3 = _split3(row)
    r = lax.broadcasted_iota(jnp.int32, (BF16_ROWS, LANES), 0)
    stacked = jnp.where(r == 0, p1.astype(F32),
                        jnp.where(r == 1, p2.astype(F32),
                                  jnp.where(r == 2, p3.astype(F32), 0.0))).astype(BF16)
    return _dot_tn(stacked, jnp.ones((BF16_ROWS, LANES), BF16))


def _rms_matmul_kernel(x_ref, g_ref, w_ref, o_ref, xn_ref):
    @pl.when(pl.program_id(1) == 0)
    def _():
        xn_ref[...] = _rms(x_ref[...], g_ref[...]).astype(BF16)

    o_ref[...] = _dot(xn_ref[...], w_ref[...].astype(BF16))


def _in_proj_kernel(x_ref, g_ref, w_ref, wt_ref, o_ref, ot_ref, xn_ref, *, n_tail):
    @pl.when(pl.program_id(1) == 0)
    def _():
        xn = _rms(x_ref[...], g_ref[...]).astype(BF16)
        xn_ref[...] = xn
        row = lax.broadcasted_iota(jnp.int32, wt_ref.shape, 0)
        wt = jnp.where(row < n_tail, wt_ref[...], jnp.zeros_like(wt_ref))
        ot_ref[...] = _dot_nt(xn, wt)

    o_ref[...] = _dot_nt(xn_ref[...], w_ref[...])


def _out_proj_kernel(a_ref, w_ref, x_ref, g_ref, gq_ref, wq_ref, o_ref, q_ref):
    x1 = x_ref[...] + _rms(_dot(a_ref[...].astype(BF16), w_ref[...]), g_ref[...])
    o_ref[...] = x1
    q_ref[...] = _dot(_rms(x1, gq_ref[...]).astype(BF16), wq_ref[...])


def _mlp_kernel(x_ref, g1_ref, w1_ref, w2_ref, g2_ref, o_ref, xn_ref, *, n_chunk):
    f = pl.program_id(1)

    @pl.when(f == 0)
    def _():
        xn_ref[...] = _rms(x_ref[...], g1_ref[...]).astype(BF16)
        o_ref[...] = jnp.zeros_like(o_ref)

    h = _dot(xn_ref[...], w1_ref[...].astype(BF16))
    h = jnp.square(jnp.maximum(h, 0.0)).astype(BF16)
    width = o_ref.shape[1] // n_chunk
    for c in range(n_chunk):
        cs = pl.ds(c * width, width)
        o_ref[:, cs] += _dot(h, w2_ref[:, cs].astype(BF16))

    @pl.when(f == pl.num_programs(1) - 1)
    def _():
        o_ref[...] = x_ref[...] + _rms(o_ref[...], g2_ref[...])


def _final_rms_kernel(x_ref, g_ref, o_ref):
    o_ref[...] = _rms(x_ref[...], g_ref[...])


def _lower_bound(lb_all, layer):
    depth = lb_all.shape[0]
    rows = [lb_all[i] for i in range(depth)]
    m = functools.reduce(jnp.maximum, rows)
    es = [jnp.exp(r - m) for r in rows]
    tot = functools.reduce(lambda a, b: a + b, es)
    lb = jnp.zeros_like(rows[0])
    for i in range(1, layer + 1):
        lb = lb + es[i] / tot
    return lb


def _hgrn_kernel(*refs, layer, seg, carry, n_alias):
    q_ref, f_ref, i_ref, g_ref, lb_ref, on_ref = refs[:6]
    st_in_ref = None if carry else refs[6]
    rest = refs[(6 if carry else 7) + n_alias:]
    o_ref, st_out_ref = rest[:2]
    scratch = rest[2:]
    s_scr = None
    if carry:
        s_scr, *scratch = scratch
    qt_scr, kd_scr, bt_scr, oi_scr, v_scr, ob_scr, q_scr, k_scr, b_scr, vh_scr = scratch

    rows, width = q_ref.shape
    heads = width // LANES
    nseg = rows // seg

    fr = f_ref[...]
    qr = q_ref[...]
    t = jnp.exp(-jnp.abs(fr))
    log_sig = jnp.minimum(fr, 0.0) - jnp.log1p(t)
    sig_neg = jnp.where(fr >= 0.0, t, 1.0) / (1.0 + t)
    if layer == 0:
        logf, k = log_sig, sig_neg
    else:
        lb = _lower_bound(lb_ref[...], layer)
        l1 = jnp.log(lb)
        l2 = jnp.log1p(-lb) + log_sig
        logf = jnp.maximum(l1, l2) + jnp.log1p(jnp.exp(-jnp.abs(l1 - l2)))
        k = (1.0 - lb) * sig_neg
    q = qr * _sigmoid(qr)
    v = i_ref[...]

    same, tri = _segment_masks(rows, seg)
    lf3 = _split3(logf)
    b = _dot3(_onehot_bf16(tri), lf3)
    bt = _dot3(_onehot_bf16(same), lf3)
    qt_scr[...] = q * jnp.exp(b)
    kd_scr[...] = k * jnp.exp(bt - b)
    bt_scr[...] = bt
    v_scr[...] = v

    ones_b = jnp.ones((LANES, LANES), BF16)
    for h in range(heads):
        cs = slice(h * LANES, (h + 1) * LANES)
        for ref, val in ((q_scr, q), (k_scr, k), (b_scr, b), (vh_scr, v)):
            ref[h] = val[:, cs]

        def pieces(ref):
            return [ref[h, pl.ds(tau, nseg, stride=seg), :] for tau in range(seg)]

        q_t, k_t, b_t, v_t = pieces(q_scr), pieces(k_scr), pieces(b_scr), pieces(vh_scr)
        acc = [None] * seg
        for d in range(seg):
            prod = jnp.concatenate(
                [q_t[tau] * k_t[tau - d] * jnp.exp(b_t[tau] - b_t[tau - d]) for tau in range(d, seg)],
                axis=0).astype(BF16)
            w = _dot(prod, ones_b)
            for tau in range(d, seg):
                term = w[(tau - d) * nseg:(tau - d + 1) * nseg, :] * v_t[tau - d]
                acc[tau] = term if acc[tau] is None else acc[tau] + term
        for tau in range(seg):
            ob_scr[h, pl.ds(tau, nseg, stride=seg), :] = acc[tau]

    if carry:
        @pl.when(pl.program_id(1) == 0)
        def _():
            s_scr[...] = jnp.zeros_like(s_scr)

    def seg_body(n, c):
        r0 = pl.multiple_of(n * seg, seg)
        for h in range(heads):
            cs = slice(h * LANES, (h + 1) * LANES)
            s = s_scr[h] if carry else st_in_ref[n, h]
            qt_n = qt_scr[pl.ds(r0, seg), cs].astype(BF16)
            oi_scr[pl.ds(r0, seg), cs] = _dot(qt_n, s.astype(BF16))
            kd_n = kd_scr[pl.ds(r0, seg), cs].astype(BF16)
            v_n = v_scr[pl.ds(r0, seg), cs].astype(BF16)
            upd = _dot_tn(kd_n, v_n)
            s_new = _lane_column(jnp.exp(bt_scr[pl.ds(r0, 1), cs])) * s + upd
            if carry:
                s_scr[h] = s_new
            else:
                st_out_ref[n, h] = s_new
        return c

    lax.fori_loop(0, rows // seg, seg_body, 0, unroll=2)

    o = jnp.concatenate(
        [(lambda oh: oh * lax.rsqrt(jnp.mean(oh * oh, axis=-1, keepdims=True) + EPS))(
            ob_scr[h] + oi_scr[:, h * LANES:(h + 1) * LANES]) for h in range(heads)], axis=1)
    o_ref[...] = o * on_ref[...] * _sigmoid(g_ref[...])

    if carry:
        @pl.when(pl.program_id(1) == pl.num_programs(1) - 1)
        def _():
            st_out_ref[...] = s_scr[...]


def _ssd_kernel(*refs, seg, carry, n_alias):
    z_ref, xa_ref, xb_ref, dt_ref, cw_ref, cb_ref, hp_ref, dsk_ref, ng_ref = refs[:9]
    st_in_ref, cs_in_ref = (None, None) if carry else refs[9:11]
    rest = refs[(9 if carry else 11) + n_alias:]
    o_ref, st_out_ref, cs_out_ref = rest[:3]
    scratch = rest[3:]
    h_scr = None
    if carry:
        h_scr, *scratch = scratch
    xpad, cm_scr, bm_scr, xw_scr, cle_scr, yi_scr = scratch

    rows, dinner = z_ref.shape
    cdim = xa_ref.shape[1] + xb_ref.shape[1]
    nseg = rows // seg
    gn = (cdim - dinner) // 2
    head_pad = xpad.shape[1] - seg
    tail = CONV_W - 1

    if carry:
        @pl.when(pl.program_id(1) == 0)
        def _():
            xpad[:, pl.ds(head_pad - tail, tail), :] = jnp.zeros((nseg, tail, cdim), F32)
    else:
        xpad[:, pl.ds(head_pad - tail, tail), :] = cs_in_ref[...]
    xpad[:, pl.ds(head_pad, seg), pl.ds(0, dinner)] = xa_ref[...].reshape(nseg, seg, dinner)
    xpad[:, pl.ds(head_pad, seg), pl.ds(dinner, cdim - dinner)] = xb_ref[...].reshape(nseg, seg, cdim - dinner)
    cw = cw_ref[...]
    conv = cb_ref[...] + xpad[:, pl.ds(head_pad - tail, seg), :] * cw[0:1, :]
    for j in range(1, CONV_W):
        conv = conv + xpad[:, pl.ds(head_pad - tail + j, seg), :] * cw[j:j + 1, :]
    new_tail = xpad[:, pl.ds(head_pad + seg - tail, tail), :]
    cs_out_ref[...] = new_tail.reshape(cs_out_ref.shape)
    if carry:
        xpad[:, pl.ds(head_pad - tail, tail), :] = new_tail
    u = conv.reshape(rows, cdim)
    u = u * _sigmoid(u)
    xs = u[:, :dinner]
    bm = u[:, dinner:dinner + gn]
    cm = u[:, dinner + gn:]

    hp = hp_ref[...]
    dt = _softplus(dt_ref[...] + hp[0:1, :])
    a = dt * (-jnp.exp(hp[1:2, :]))
    same, tri = _segment_masks(rows, seg)
    a3 = _split3(a)
    cum = _dot3(_onehot_bf16(tri), a3)
    cl = _dot3(_onehot_bf16(same), a3)
    cum3 = _split3(cum)
    dt3 = _split3(dt)
    hh = lax.broadcasted_iota(jnp.int32, (BF16_ROWS, LANES), 0)
    hl = lax.broadcasted_iota(jnp.int32, (BF16_ROWS, LANES), 1)
    eye_b = _onehot_bf16(hh == hl)
    cum_t = _dot_nt(eye_b, cum3[0]) + _dot_nt(eye_b, cum3[1]) + _dot_nt(eye_b, cum3[2])
    dt_t = _dot_nt(eye_b, dt3[0]) + _dot_nt(eye_b, dt3[1]) + _dot_nt(eye_b, dt3[2])

    eh = lax.broadcasted_iota(jnp.int32, (LANES, dinner), 0)
    el = lax.broadcasted_iota(jnp.int32, (LANES, dinner), 1)
    spread_b = _onehot_bf16(lax.shift_right_logical(el, SSM_HEAD_DIM.bit_length() - 1) == eh)
    cum_e = _dot3_r(cum3, spread_b)
    cl_e = _dot3_r(_split3(cl), spread_b)
    dt_e = _dot3_r(dt3, spread_b)
    xw_scr[...] = xs * (jnp.exp(cl_e - cum_e) * dt_e)
    cle_scr[...] = cl_e
    cm_scr[...] = cm
    bm_scr[...] = bm

    lane = lax.broadcasted_iota(jnp.int32, (rows, LANES), 1)
    heads_per_group = SSM_HEADS // SSM_GROUPS
    y_parts = []
    for g in range(SSM_GROUPS):
        gs = slice(g * SSM_STATE, (g + 1) * SSM_STATE)
        scores = _dot_nt(cm[:, gs].astype(BF16), bm[:, gs].astype(BF16))
        for j in range(g * heads_per_group // 2, (g + 1) * heads_per_group // 2):
            xp = xs[:, j * LANES:(j + 1) * LANES]
            yp = jnp.zeros((rows, LANES), F32)
            for half in range(2):
                h = 2 * j + half
                decay = jnp.where(tri, jnp.exp(cum[:, h:h + 1] - cum_t[h:h + 1, :]), 0.0)
                gm = (scores * decay * dt_t[h:h + 1, :]).astype(BF16)
                in_half = (lane < SSM_HEAD_DIM) if half == 0 else (lane >= SSM_HEAD_DIM)
                yp = yp + _dot(gm, jnp.where(in_half, xp, 0.0).astype(BF16))
            y_parts.append(yp)
    y = jnp.concatenate(y_parts, axis=1)

    if carry:
        @pl.when(pl.program_id(1) == 0)
        def _():
            h_scr[...] = jnp.zeros_like(h_scr)

    def seg_body(n, c):
        r0 = pl.multiple_of(n * seg, seg)
        for j in range(SSM_PAIRS):
            g = (2 * j) // heads_per_group
            gs = slice(g * SSM_STATE, (g + 1) * SSM_STATE)
            ps = slice(j * LANES, (j + 1) * LANES)
            hst = h_scr[j] if carry else st_in_ref[n, j]
            cm_n = cm_scr[pl.ds(r0, seg), gs].astype(BF16)
            yi_scr[pl.ds(r0, seg), ps] = _dot_nt(cm_n, hst.astype(BF16))
            xw_n = xw_scr[pl.ds(r0, seg), ps].astype(BF16)
            bm_n = bm_scr[pl.ds(r0, seg), gs].astype(BF16)
            upd = _dot_tn(xw_n, bm_n)
            h_new = _lane_column(jnp.exp(cle_scr[pl.ds(r0, 1), ps])) * hst + upd
            if carry:
                h_scr[j] = h_new
            else:
                st_out_ref[n, j] = h_new
        return c

    lax.fori_loop(0, nseg, seg_body, 0, unroll=min(2, nseg))

    y = y + jnp.exp(cum_e) * yi_scr[...] + dsk_ref[...] * xs
    z = z_ref[...]
    y = y * (z * _sigmoid(z))
    gw = dinner // SSM_GROUPS
    y = jnp.concatenate(
        [(lambda yg: yg * lax.rsqrt(jnp.mean(yg * yg, axis=-1, keepdims=True) + EPS))(
            y[:, g * gw:(g + 1) * gw]) for g in range(SSM_GROUPS)], axis=1)
    o_ref[...] = y * ng_ref[...]

    if carry:
        @pl.when(pl.program_id(1) == pl.num_programs(1) - 1)
        def _():
            st_out_ref[...] = h_scr[...]


def _dot3_r(xs, m_b):
    return _dot(xs[0], m_b) + _dot(xs[1], m_b) + _dot(xs[2], m_b)


def _attn_kernel(q_ref, k_ref, v_ref, x_ref, wo_ref, g_ref, *rest, seg):
    o_ref, att_scr = rest[-2:]
    rows = q_ref.shape[0]
    scale = 1.0 / math.sqrt(X_HEAD_DIM)

    def seg_body(n, c):
        r0 = pl.multiple_of(n * seg, seg)
        for h in range(X_HEADS):
            cs = slice(h * X_HEAD_DIM, (h + 1) * X_HEAD_DIM)
            qh = q_ref[pl.ds(r0, seg), cs].astype(BF16)
            s = _dot_nt(qh, k_ref[n, :, cs].astype(BF16)) * scale
            e = jnp.exp(s - jnp.max(s, axis=-1, keepdims=True))
            p = e / jnp.sum(e, axis=-1, keepdims=True)
            att_scr[pl.ds(r0, seg), cs] = _dot(p.astype(BF16), v_ref[n, :, cs].astype(BF16))
        return c

    lax.fori_loop(0, rows // seg, seg_body, 0)
    att = _dot(att_scr[...].astype(BF16), wo_ref[...])
    o_ref[...] = x_ref[...] + _rms(att, g_ref[...])


def _attn_interleaved_kernel(q_ref, k_ref, v_ref, x_ref, wo_ref, g_ref, *rest, seg):
    o_ref, att_scr = rest[-2:]
    rows = q_ref.shape[0]
    nkv = k_ref.shape[1]
    scale = 1.0 / math.sqrt(X_HEAD_DIM)
    rr = lax.broadcasted_iota(jnp.int32, (X_HEADS * seg, nkv), 0)
    cc = lax.broadcasted_iota(jnp.int32, (X_HEADS * seg, nkv), 1)
    own_head = (cc & (X_HEADS - 1)) == lax.shift_right_logical(rr, seg.bit_length() - 1)

    def seg_body(n, c):
        r0 = pl.multiple_of(n * seg, seg)
        qs = jnp.concatenate(
            [q_ref[pl.ds(r0, seg), h * X_HEAD_DIM:(h + 1) * X_HEAD_DIM] for h in range(X_HEADS)],
            axis=0).astype(BF16)
        s = jnp.where(own_head, _dot_nt(qs, k_ref[n].astype(BF16)) * scale, -jnp.inf)
        e = jnp.exp(s - jnp.max(s, axis=-1, keepdims=True))
        p = e / jnp.sum(e, axis=-1, keepdims=True)
        o = _dot(p.astype(BF16), v_ref[n].astype(BF16))
        for h in range(X_HEADS):
            att_scr[pl.ds(r0, seg), h * X_HEAD_DIM:(h + 1) * X_HEAD_DIM] = o[h * seg:(h + 1) * seg]
        return c

    lax.fori_loop(0, rows // seg, seg_body, 0, unroll=2)
    att = _dot(att_scr[...].astype(BF16), wo_ref[...])
    o_ref[...] = x_ref[...] + _rms(att, g_ref[...])


def _params(n_grid):
    return pltpu.CompilerParams(dimension_semantics=("arbitrary",) * n_grid,
                                vmem_limit_bytes=VMEM_LIMIT)


_ANY = pl.BlockSpec(memory_space=pl.ANY)


def _vec3(p):
    return p.reshape(p.shape[0], 1, p.shape[1])


def _layer_row(n):
    return lambda layer: pl.BlockSpec((None, 1, n), lambda *_: (layer, 0, 0))


def _in_proj(x, g3, w_in_t, layer, *, tm, tn):
    m, d = x.shape
    n_all = w_in_t.shape[1]
    n_main = (n_all // LANES) * LANES
    n_tail = n_all - n_main
    return pl.pallas_call(
        functools.partial(_in_proj_kernel, n_tail=n_tail),
        grid=(m // tm, n_main // tn),
        in_specs=[
            pl.BlockSpec((tm, d), lambda i, j: (i, 0)),
            pl.BlockSpec((None, 1, d), lambda i, j: (layer, 0, 0)),
            pl.BlockSpec((None, tn, d), lambda i, j: (layer, j, 0)),
            pl.BlockSpec((None, LANES, d), lambda i, j: (layer, n_main // LANES, 0)),
        ],
        out_specs=[
            pl.BlockSpec((tm, tn), lambda i, j: (i, j)),
            pl.BlockSpec((tm, LANES), lambda i, j: (i, 0)),
        ],
        out_shape=[jax.ShapeDtypeStruct((m, n_main), F32), jax.ShapeDtypeStruct((m, LANES), F32)],
        scratch_shapes=[pltpu.VMEM((tm, d), BF16)],
        compiler_params=_params(2),
        name=f"in_proj_l{layer}",
    )(x, g3, w_in_t, w_in_t)


def _rms_matmul(x, g3, w, layer, *, tm, tn, name, stacked_prev=None, stacked=False):
    m, d = x.shape
    depth, _, n = w.shape
    in_specs = [
        pl.BlockSpec((tm, d), lambda i, j: (i, 0)),
        pl.BlockSpec((None, 1, d), lambda i, j: (layer, 0, 0)),
        pl.BlockSpec((None, d, tn), lambda i, j: (layer, 0, j)),
    ]
    args = [x, g3, w]
    aliases = {}
    if stacked:
        out_spec = pl.BlockSpec((None, tm, tn), lambda i, j: (layer, i, j))
        out_shape = jax.ShapeDtypeStruct((depth, m, n), F32)
        if stacked_prev is not None:
            in_specs.append(_ANY)
            args.append(stacked_prev)
            aliases = {3: 0}
    else:
        out_spec = pl.BlockSpec((tm, tn), lambda i, j: (i, j))
        out_shape = jax.ShapeDtypeStruct((m, n), F32)

    def body(x_ref, g_ref, w_ref, *rest):
        o_ref, xn_ref = rest[-2:]
        _rms_matmul_kernel(x_ref, g_ref, w_ref, o_ref, xn_ref)

    return pl.pallas_call(
        body,
        grid=(m // tm, n // tn),
        in_specs=in_specs,
        out_specs=out_spec,
        out_shape=out_shape,
        scratch_shapes=[pltpu.VMEM((tm, d), BF16)],
        input_output_aliases=aliases,
        compiler_params=_params(2),
        name=f"{name}_l{layer}",
    )(*args)


def _out_proj(a, w_b, x, g3, gq3, wq_b, layer, *, tm):
    m, kdim = a.shape
    n = w_b.shape[2]
    nq = wq_b.shape[2]
    return pl.pallas_call(
        _out_proj_kernel,
        grid=(m // tm,),
        in_specs=[
            pl.BlockSpec((tm, kdim), lambda i: (i, 0)),
            pl.BlockSpec((None, kdim, n), lambda i: (layer, 0, 0)),
            pl.BlockSpec((tm, n), lambda i: (i, 0)),
            pl.BlockSpec((None, 1, n), lambda i: (layer, 0, 0)),
            pl.BlockSpec((None, 1, n), lambda i: (layer, 0, 0)),
            pl.BlockSpec((None, n, nq), lambda i: (layer, 0, 0)),
        ],
        out_specs=[pl.BlockSpec((tm, n), lambda i: (i, 0)),
                   pl.BlockSpec((tm, nq), lambda i: (i, 0))],
        out_shape=[jax.ShapeDtypeStruct((m, n), F32), jax.ShapeDtypeStruct((m, nq), F32)],
        compiler_params=_params(1),
        name=f"out_proj_l{layer}",
    )(a, w_b, x, g3, gq3, wq_b)


def _mlp(x, g1, w1, w2, g2, layer, *, tm, tf):
    m, d = x.shape
    dff = w1.shape[2]
    single = pl.Buffered(1)
    return pl.pallas_call(
        functools.partial(_mlp_kernel, n_chunk=4),
        grid=(m // tm, dff // tf),
        in_specs=[
            pl.BlockSpec((tm, d), lambda i, f: (i, 0), pipeline_mode=single),
            pl.BlockSpec((None, 1, d), lambda i, f: (layer, 0, 0)),
            pl.BlockSpec((None, d, tf), lambda i, f: (layer, 0, f)),
            pl.BlockSpec((None, tf, d), lambda i, f: (layer, f, 0)),
            pl.BlockSpec((None, 1, d), lambda i, f: (layer, 0, 0)),
        ],
        out_specs=pl.BlockSpec((tm, d), lambda i, f: (i, 0)),
        out_shape=jax.ShapeDtypeStruct((m, d), F32),
        scratch_shapes=[pltpu.VMEM((tm, d), BF16)],
        compiler_params=_params(2),
        name=f"mlp_l{layer}",
    )(x, g1, w1, w2, g2)


def _final_rms(x, g2, row0, nrows, *, tm, name):
    d = x.shape[1]
    return pl.pallas_call(
        _final_rms_kernel,
        grid=(nrows // tm,),
        in_specs=[
            pl.BlockSpec((tm, d), lambda i: (row0 // tm + i, 0)),
            pl.BlockSpec((1, d), lambda i: (0, 0)),
        ],
        out_specs=pl.BlockSpec((tm, d), lambda i: (i, 0)),
        out_shape=jax.ShapeDtypeStruct((nrows, d), F32),
        compiler_params=_params(1),
        name=name,
    )(x, g2)


def _hgrn(proj, lb3, on3, layer, *, row0, nseq, seqlen, rows, seg, mix_shape, mix_prev, st_in, st_prev,
          st_shape, name):
    carry = st_in is None
    width = HG_HEADS * HG_DIM
    depth = lb3.shape[0]
    if carry:
        nt = seqlen // rows
        grid = (nseq, nt)
        rmap = lambda b, t: row0 // rows + b * nt + t
        st_block = (None, None, HG_HEADS, HG_DIM, HG_DIM)
        st_map = lambda b, t: (layer, b, 0, 0, 0)
    else:
        per = rows // seg
        grid = (nseq // per, 1)
        rmap = lambda s, t: row0 // rows + s
        st_block = (None, per, HG_HEADS, HG_DIM, HG_DIM)
        st_map = lambda s, t: (layer, s, 0, 0, 0)

    col = lambda c: pl.BlockSpec((rows, width), lambda *ix: (rmap(*ix), c))
    in_specs = [col(0), col(1), col(2), col(3),
                pl.BlockSpec((depth, 1, width), lambda *ix: (0, 0, 0)),
                pl.BlockSpec((None, 1, width), lambda *ix: (layer, 0, 0))]
    args = [proj, proj, proj, proj, lb3, on3]
    if not carry:
        in_specs.append(pl.BlockSpec(st_block, st_map))
        args.append(st_in)
    n_fixed = len(args)
    aliases = {}
    if mix_prev is not None:
        aliases[len(args)] = 0
        in_specs.append(_ANY)
        args.append(mix_prev)
    if st_prev is not None:
        aliases[len(args)] = 1
        in_specs.append(_ANY)
        args.append(st_prev)

    tile = pltpu.VMEM((rows, width), F32)
    per_head = pltpu.VMEM((HG_HEADS, rows, HG_DIM), F32)
    scratch = [tile] * 5 + [per_head] * 5
    if carry:
        scratch = [pltpu.VMEM((HG_HEADS, HG_DIM, HG_DIM), F32)] + scratch
    return pl.pallas_call(
        functools.partial(_hgrn_kernel, layer=layer, seg=seg, carry=carry, n_alias=len(args) - n_fixed),
        grid=grid,
        in_specs=in_specs,
        out_specs=[pl.BlockSpec((rows, width), lambda *ix: (rmap(*ix), 0)),
                   pl.BlockSpec(st_block, st_map)],
        out_shape=[jax.ShapeDtypeStruct(mix_shape, F32), jax.ShapeDtypeStruct(st_shape, F32)],
        scratch_shapes=scratch,
        input_output_aliases=aliases,
        compiler_params=_params(2),
        name=f"{name}_l{layer}",
    )(*args)


def _ssd(proj, dt, cw, cb3, hp, dsk3, ng3, layer, *, row0, nseq, seqlen, rows, seg, mix_prev, st_in, cs_in,
         st_prev, cs_prev, st_shape, cs_shape, name):
    carry = st_in is None
    dinner = SSM_HEADS * SSM_HEAD_DIM
    cdim = dinner + 2 * SSM_GROUPS * SSM_STATE
    assert (4 * HG_HEADS * HG_DIM) % dinner == 0 and cdim == 2 * dinner
    z_col = (4 * HG_HEADS * HG_DIM) // dinner
    tail = CONV_W - 1
    if carry:
        nt = seqlen // rows
        grid = (nseq, nt)
        rmap = lambda b, t: row0 // rows + b * nt + t
        st_block = (None, None, SSM_PAIRS, LANES, SSM_STATE)
        cs_block = (None, None, tail, cdim)
        st_map = lambda b, t: (layer, b, 0, 0, 0)
        cs_map = lambda b, t: (layer, b, 0, 0)
    else:
        per = rows // seg
        grid = (nseq // per, 1)
        rmap = lambda s, t: row0 // rows + s
        st_block = (None, per, SSM_PAIRS, LANES, SSM_STATE)
        cs_block = (None, per, tail, cdim)
        st_map = lambda s, t: (layer, s, 0, 0, 0)
        cs_map = lambda s, t: (layer, s, 0, 0)

    in_specs = [
        pl.BlockSpec((rows, dinner), lambda *ix: (rmap(*ix), z_col)),
        pl.BlockSpec((rows, dinner), lambda *ix: (rmap(*ix), z_col + 1)),
        pl.BlockSpec((rows, dinner), lambda *ix: (rmap(*ix), z_col + 2)),
        pl.BlockSpec((rows, LANES), lambda *ix: (rmap(*ix), 0)),
        pl.BlockSpec((None, CONV_W, cdim), lambda *ix: (layer, 0, 0)),
        pl.BlockSpec((None, 1, cdim), lambda *ix: (layer, 0, 0)),
        pl.BlockSpec((None, 8, LANES), lambda *ix: (layer, 0, 0)),
        pl.BlockSpec((None, 1, dinner), lambda *ix: (layer, 0, 0)),
        pl.BlockSpec((None, 1, dinner), lambda *ix: (layer, 0, 0)),
    ]
    args = [proj, proj, proj, dt, cw, cb3, hp, dsk3, ng3]
    if not carry:
        in_specs += [pl.BlockSpec(st_block, st_map), pl.BlockSpec(cs_block, cs_map)]
        args += [st_in, cs_in]
    n_fixed = len(args)
    aliases = {len(args): 0}
    in_specs.append(_ANY)
    args.append(mix_prev)
    if st_prev is not None:
        aliases[len(args)] = 1
        aliases[len(args) + 1] = 2
        in_specs += [_ANY, _ANY]
        args += [st_prev, cs_prev]

    nseg = rows // seg
    scratch = [pltpu.VMEM((nseg, seg + 8, cdim), F32),
               pltpu.VMEM((rows, SSM_GROUPS * SSM_STATE), F32),
               pltpu.VMEM((rows, SSM_GROUPS * SSM_STATE), F32),
               pltpu.VMEM((rows, dinner), F32),
               pltpu.VMEM((rows, dinner), F32),
               pltpu.VMEM((rows, dinner), F32)]
    if carry:
        scratch = [pltpu.VMEM((SSM_PAIRS, LANES, SSM_STATE), F32)] + scratch
    mix_shape = mix_prev.shape
    return pl.pallas_call(
        functools.partial(_ssd_kernel, seg=seg, carry=carry, n_alias=len(args) - n_fixed),
        grid=grid,
        in_specs=in_specs,
        out_specs=[pl.BlockSpec((rows, dinner), lambda *ix: (rmap(*ix), HG_HEADS * HG_DIM // dinner)),
                   pl.BlockSpec(st_block, st_map),
                   pl.BlockSpec(cs_block, cs_map)],
        out_shape=[jax.ShapeDtypeStruct(mix_shape, F32), jax.ShapeDtypeStruct(st_shape, F32),
                   jax.ShapeDtypeStruct(cs_shape, F32)],
        scratch_shapes=scratch,
        input_output_aliases=aliases,
        compiler_params=_params(2),
        name=f"{name}_l{layer}",
    )(*args)


def _attn(q, mk, mv, x, wo, g3, layer, *, row0, nrows, seqlen, rows, seg, x_prev, name):
    m, d = x.shape
    xdim = q.shape[1]
    interleaved = mk.shape[3] == X_HEAD_DIM
    body = _attn_interleaved_kernel if interleaved else _attn_kernel
    per = rows // seg
    base = row0 // rows
    assert seg == min(rows, seqlen) and seqlen % seg == 0
    tiles_per_seq = seqlen // seg if per == 1 else 1
    kv_map = lambda i: (layer, i // tiles_per_seq, 0, 0)
    in_specs = [
        pl.BlockSpec((rows, xdim), lambda i: (base + i, 0)),
        pl.BlockSpec((None, per) + mk.shape[2:], kv_map),
        pl.BlockSpec((None, per) + mv.shape[2:], kv_map),
        pl.BlockSpec((rows, d), lambda i: (base + i, 0)),
        pl.BlockSpec((None, xdim, d), lambda i: (layer, 0, 0)),
        pl.BlockSpec((None, 1, d), lambda i: (layer, 0, 0)),
    ]
    args = [q, mk, mv, x, wo, g3]
    aliases = {}
    if x_prev is not None:
        aliases = {len(args): 0}
        in_specs.append(_ANY)
        args.append(x_prev)
    return pl.pallas_call(
        functools.partial(body, seg=seg),
        grid=(nrows // rows,),
        in_specs=in_specs,
        out_specs=pl.BlockSpec((rows, d), lambda i: (base + i, 0)),
        out_shape=jax.ShapeDtypeStruct((m, d), F32),
        scratch_shapes=[pltpu.VMEM((rows, xdim), F32)],
        input_output_aliases=aliases,
        compiler_params=_params(1),
        name=f"{name}_l{layer}",
    )(*args)


def kernel(x_prompt, x_sample, mem_prompt, state_hgrn, state_ssm, state_conv, cache_mem_k, cache_mem_v, g_pre_mix, g_post_mix, g_pre_x, g_post_x, g_pre_mlp, g_post_mlp, g_mem, g_final, w_in, hg_lb, hg_onorm, conv_w, conv_b, dt_bias, a_log, d_skip, ssm_norm, w_out, wq_x, wk_x, wv_x, wo_x, w_mlp1, w_mlp2):
    bp, lp, d = x_prompt.shape
    bs, ls, _ = x_sample.shape
    depth = w_in.shape[0]
    n_mem = mem_prompt.shape[1]
    mp, ms = bp * lp, bs * ls
    m = mp + ms
    xdim = X_HEADS * X_HEAD_DIM
    hg_width = HG_HEADS * HG_DIM
    dinner = SSM_HEADS * SSM_HEAD_DIM
    cdim = conv_w.shape[2]
    mix_shape = (m, hg_width + dinner)

    x = jnp.concatenate([x_prompt.reshape(mp, d), x_sample.reshape(ms, d)], axis=0)
    mem = mem_prompt.reshape(bp * n_mem, d)

    g_pre_mix3, g_post_mix3, g_pre_x3, g_post_x3 = map(_vec3, (g_pre_mix, g_post_mix, g_pre_x, g_post_x))
    g_pre_mlp3, g_post_mlp3, g_mem3 = map(_vec3, (g_pre_mlp, g_post_mlp, g_mem))
    lb3, on3, cb3, ng3 = map(_vec3, (hg_lb, hg_onorm, conv_b, ssm_norm))
    dsk3 = _vec3(jnp.repeat(d_skip, SSM_HEAD_DIM, axis=1))
    hp = jnp.zeros((depth, 8, LANES), F32)
    hp = hp.at[:, 0, :SSM_HEADS].set(dt_bias).at[:, 1, :SSM_HEADS].set(a_log)

    st_ssm_in = state_ssm.reshape(depth, bs, SSM_PAIRS, LANES, SSM_STATE)
    ck = cache_mem_k.reshape(depth, bs, n_mem * X_HEADS, X_HEAD_DIM)
    cv = cache_mem_v.reshape(depth, bs, n_mem * X_HEADS, X_HEAD_DIM)

    w_in_t = jnp.swapaxes(w_in, 1, 2).astype(BF16)
    w_out_b, wq_b, wo_b = w_out.astype(BF16), wq_x.astype(BF16), wo_x.astype(BF16)

    p_hg = p_ssm = p_conv = p_mk = p_mv = s_hg = s_ssm = s_conv = None
    p_hg_shape = (depth, bp, HG_HEADS, HG_DIM, HG_DIM)
    s_hg_shape = (depth, bs, HG_HEADS, HG_DIM, HG_DIM)
    p_ssm_shape = (depth, bp, SSM_PAIRS, LANES, SSM_STATE)
    s_ssm_shape = (depth, bs, SSM_PAIRS, LANES, SSM_STATE)
    p_conv_shape = (depth, bp, CONV_W - 1, cdim)
    s_conv_shape = (depth, bs, CONV_W - 1, cdim)

    tile = 128
    hg_seg = 16

    for layer in range(depth):
        proj, dt = _in_proj(x, g_pre_mix3, w_in_t, layer, tm=1024, tn=1024)

        mix, p_hg = _hgrn(proj, lb3, on3, layer, row0=0, nseq=bp, seqlen=lp, rows=tile, seg=hg_seg,
                          mix_shape=mix_shape, mix_prev=None, st_in=None, st_prev=p_hg,
                          st_shape=p_hg_shape, name="hgrn_prompt")
        mix, s_hg = _hgrn(proj, lb3, on3, layer, row0=mp, nseq=bs, seqlen=ls, rows=tile, seg=ls,
                          mix_shape=mix_shape, mix_prev=mix, st_in=state_hgrn, st_prev=s_hg,
                          st_shape=s_hg_shape, name="hgrn_sample")
        mix, p_ssm, p_conv = _ssd(proj, dt, conv_w, cb3, hp, dsk3, ng3, layer, row0=0, nseq=bp, seqlen=lp,
                                  rows=tile, seg=tile, mix_prev=mix, st_in=None, cs_in=None,
                                  st_prev=p_ssm, cs_prev=p_conv, st_shape=p_ssm_shape,
                                  cs_shape=p_conv_shape, name="ssd_prompt")
        mix, s_ssm, s_conv = _ssd(proj, dt, conv_w, cb3, hp, dsk3, ng3, layer, row0=mp, nseq=bs, seqlen=ls,
                                  rows=tile, seg=ls, mix_prev=mix, st_in=st_ssm_in, cs_in=state_conv,
                                  st_prev=s_ssm, cs_prev=s_conv, st_shape=s_ssm_shape,
                                  cs_shape=s_conv_shape, name="ssd_sample")

        x, q = _out_proj(mix, w_out_b, x, g_post_mix3, g_pre_x3, wq_b, layer, tm=512)
        p_mk = _rms_matmul(mem, g_mem3, wk_x, layer, tm=bp * n_mem, tn=xdim, name="mem_k",
                           stacked=True, stacked_prev=p_mk)
        p_mv = _rms_matmul(mem, g_mem3, wv_x, layer, tm=bp * n_mem, tn=xdim, name="mem_v",
                           stacked=True, stacked_prev=p_mv)
        mk4 = p_mk.reshape(depth, bp, n_mem, xdim)
        mv4 = p_mv.reshape(depth, bp, n_mem, xdim)
        x2 = _attn(q, mk4, mv4, x, wo_b, g_post_x3, layer, row0=0, nrows=mp, seqlen=lp, rows=512, seg=512,
                   x_prev=None, name="attn_prompt")
        x = _attn(q, ck, cv, x, wo_b, g_post_x3, layer, row0=mp, nrows=ms, seqlen=ls, rows=tile, seg=ls,
                  x_prev=x2, name="attn_sample")

        x = _mlp(x, g_pre_mlp3, w_mlp1, w_mlp2, g_post_mlp3, layer, tm=1024, tf=512)

    g_fin = g_final.reshape(1, d)
    y_prompt = _final_rms(x, g_fin, 0, mp, tm=512, name="final_prompt").reshape(bp, lp, d)
    y_sample = _final_rms(x, g_fin, mp, ms, tm=512, name="final_sample").reshape(bs, ls, d)

    return (y_prompt, y_sample,
            p_hg,
            p_ssm.reshape(depth, bp, SSM_HEADS, SSM_HEAD_DIM, SSM_STATE),
            p_conv,
            p_mk.reshape(depth, bp, n_mem, X_HEADS, X_HEAD_DIM),
            p_mv.reshape(depth, bp, n_mem, X_HEADS, X_HEAD_DIM),
            s_hg,
            s_ssm.reshape(depth, bs, SSM_HEADS, SSM_HEAD_DIM, SSM_STATE),
            s_conv)
```

```python
import functools
import math

import jax
import jax.numpy as jnp
from jax import lax
from jax.experimental import pallas as pl
from jax.experimental.pallas import tpu as pltpu

F32 = jnp.float32
BF16 = jnp.bfloat16
EPS = 1e-6

LANES = 128
BF16_ROWS = 16
VMEM_LIMIT = 56 * 1024 * 1024

HG_HEADS = 8
HG_DIM = 128
SSM_HEADS = 16
SSM_HEAD_DIM = 64
SSM_GROUPS = 4
SSM_STATE = 128
SSM_PAIRS = SSM_HEADS // 2
CONV_W = 4
X_HEADS = 4
X_HEAD_DIM = 128


def _dot(a, b):
    return jnp.dot(a, b, preferred_element_type=F32)


def _dot_nt(a, b):
    return lax.dot_general(a, b, (((1,), (1,)), ((), ())), preferred_element_type=F32)


def _dot_tn(a, b):
    return lax.dot_general(a, b, (((0,), (0,)), ((), ())), preferred_element_type=F32)


def _split3(x):
    x1 = x.astype(BF16)
    r = x - x1.astype(F32)
    x2 = r.astype(BF16)
    r = r - x2.astype(F32)
    return x1, x2, r.astype(BF16)


def _dot3(m_b, xs):
    return _dot(m_b, xs[0]) + _dot(m_b, xs[1]) + _dot(m_b, xs[2])


def _rms(x, g):
    ms = jnp.mean(x * x, axis=-1, keepdims=True)
    return x * lax.rsqrt(ms + EPS) * g


def _sigmoid(x):
    return jax.nn.sigmoid(x)


def _log_sigmoid(x):
    return jnp.minimum(x, 0.0) - jnp.log1p(jnp.exp(-jnp.abs(x)))


def _softplus(x):
    return jnp.maximum(x, 0.0) + jnp.log1p(jnp.exp(-jnp.abs(x)))


def _onehot_bf16(mask):
    return jnp.where(mask, 1.0, 0.0).astype(BF16)


def _segment_masks(rows, seg):
    shift = seg.bit_length() - 1
    r = lax.broadcasted_iota(jnp.int32, (rows, rows), 0)
    c = lax.broadcasted_iota(jnp.int32, (rows, rows), 1)
    same = lax.shift_right_logical(r, shift) == lax.shift_right_logical(c, shift)
    return same, same & (c <= r)


def _lane_column(row):
    p1, p2, p3 = _split3(row)
    r = lax.broadcasted_iota(jnp.int32, (BF16_ROWS, LANES), 0)
    stacked = jnp.where(r == 0, p1.astype(F32),
                        jnp.where(r == 1, p2.astype(F32),
                                  jnp.where(r == 2, p3.astype(F32), 0.0))).astype(BF16)
    return _dot_tn(stacked, jnp.ones((BF16_ROWS, LANES), BF16))


def _rms_matmul_kernel(x_ref, g_ref, w_ref, o_ref, xn_ref):
    @pl.when(pl.program_id(1) == 0)
    def _():
        xn_ref[...] = _rms(x_ref[...], g_ref[...]).astype(BF16)

    o_ref[...] = _dot(xn_ref[...], w_ref[...].astype(BF16))


def _in_proj_kernel(x_ref, g_ref, w_ref, wt_ref, o_ref, ot_ref, xn_ref, *, n_tail):
    @pl.when(pl.program_id(1) == 0)
    def _():
        xn = _rms(x_ref[...], g_ref[...]).astype(BF16)
        xn_ref[...] = xn
        row = lax.broadcasted_iota(jnp.int32, wt_ref.shape, 0)
        wt = jnp.where(row < n_tail, wt_ref[...], jnp.zeros_like(wt_ref))
        ot_ref[...] = _dot_nt(xn, wt)

    o_ref[...] = _dot_nt(xn_ref[...], w_ref[...])


def _out_proj_kernel(a_ref, w_ref, x_ref, g_ref, gq_ref, wq_ref, o_ref, q_ref):
    x1 = x_ref[...] + _rms(_dot(a_ref[...].astype(BF16), w_ref[...]), g_ref[...])
    o_ref[...] = x1
    q_ref[...] = _dot(_rms(x1, gq_ref[...]).astype(BF16), wq_ref[...])


def _mlp_kernel(x_ref, g1_ref, w1_ref, w2_ref, g2_ref, o_ref, xn_ref, *, n_chunk):
    f = pl.program_id(1)

    @pl.when(f == 0)
    def _():
        xn_ref[...] = _rms(x_ref[...], g1_ref[...]).astype(BF16)
        o_ref[...] = jnp.zeros_like(o_ref)

    h = _dot(xn_ref[...], w1_ref[...].astype(BF16))
    h = jnp.square(jnp.maximum(h, 0.0)).astype(BF16)
    width = o_ref.shape[1] // n_chunk
    for c in range(n_chunk):
        cs = pl.ds(c * width, width)
        o_ref[:, cs] += _dot(h, w2_ref[:, cs].astype(BF16))

    @pl.when(f == pl.num_programs(1) - 1)
    def _():
        o_ref[...] = x_ref[...] + _rms(o_ref[...], g2_ref[...])


def _final_rms_kernel(x_ref, g_ref, o_ref):
    o_ref[...] = _rms(x_ref[...], g_ref[...])


def _lower_bound(lb_all, layer):
    depth = lb_all.shape[0]
    rows = [lb_all[i] for i in range(depth)]
    m = functools.reduce(jnp.maximum, rows)
    es = [jnp.exp(r - m) for r in rows]
    tot = functools.reduce(lambda a, b: a + b, es)
    lb = jnp.zeros_like(rows[0])
    for i in range(1, layer + 1):
        lb = lb + es[i] / tot
    return lb


def _hgrn_kernel(*refs, layer, seg, carry, n_alias):
    q_ref, f_ref, i_ref, g_ref, lb_ref, on_ref = refs[:6]
    st_in_ref = None if carry else refs[6]
    rest = refs[(6 if carry else 7) + n_alias:]
    o_ref, st_out_ref = rest[:2]
    scratch = rest[2:]
    s_scr = None
    if carry:
        s_scr, *scratch = scratch
    qt_scr, kd_scr, bt_scr, oi_scr, v_scr, ob_scr, q_scr, k_scr, b_scr, vh_scr = scratch

    rows, width = q_ref.shape
    heads = width // LANES
    nseg = rows // seg

    fr = f_ref[...]
    qr = q_ref[...]
    t = jnp.exp(-jnp.abs(fr))
    log_sig = jnp.minimum(fr, 0.0) - jnp.log1p(t)
    sig_neg = jnp.where(fr >= 0.0, t, 1.0) / (1.0 + t)
    if layer == 0:
        logf, k = log_sig, sig_neg
    else:
        lb = _lower_bound(lb_ref[...], layer)
        l1 = jnp.log(lb)
        l2 = jnp.log1p(-lb) + log_sig
        logf = jnp.maximum(l1, l2) + jnp.log1p(jnp.exp(-jnp.abs(l1 - l2)))
        k = (1.0 - lb) * sig_neg
    q = qr * _sigmoid(qr)
    v = i_ref[...]

    same, tri = _segment_masks(rows, seg)
    lf3 = _split3(logf)
    b = _dot3(_onehot_bf16(tri), lf3)
    bt = _dot3(_onehot_bf16(same), lf3)
    qt_scr[...] = q * jnp.exp(b)
    kd_scr[...] = k * jnp.exp(bt - b)
    bt_scr[...] = bt
    v_scr[...] = v

    ones_b = jnp.ones((LANES, LANES), BF16)
    for h in range(heads):
        cs = slice(h * LANES, (h + 1) * LANES)
        for ref, val in ((q_scr, q), (k_scr, k), (b_scr, b), (vh_scr, v)):
            ref[h] = val[:, cs]

        def pieces(ref):
            return [ref[h, pl.ds(tau, nseg, stride=seg), :] for tau in range(seg)]

        q_t, k_t, b_t, v_t = pieces(q_scr), pieces(k_scr), pieces(b_scr), pieces(vh_scr)
        acc = [None] * seg
        for d in range(seg):
            prod = jnp.concatenate(
                [q_t[tau] * k_t[tau - d] * jnp.exp(b_t[tau] - b_t[tau - d]) for tau in range(d, seg)],
                axis=0).astype(BF16)
            w = _dot(prod, ones_b)
            for tau in range(d, seg):
                term = w[(tau - d) * nseg:(tau - d + 1) * nseg, :] * v_t[tau - d]
                acc[tau] = term if acc[tau] is None else acc[tau] + term
        for tau in range(seg):
            ob_scr[h, pl.ds(tau, nseg, stride=seg), :] = acc[tau]

    if carry:
        @pl.when(pl.program_id(1) == 0)
        def _():
            s_scr[...] = jnp.zeros_like(s_scr)

    def seg_body(n, c):
        r0 = pl.multiple_of(n * seg, seg)
        for h in range(heads):
            cs = slice(h * LANES, (h + 1) * LANES)
            qt_n = qt_scr[pl.ds(r0, seg), cs].astype(BF16)
            kd_n = kd_scr[pl.ds(r0, seg), cs].astype(BF16)
            v_n = v_scr[pl.ds(r0, seg), cs].astype(BF16)
            dec = jnp.exp(bt_scr[pl.ds(r0, 1), cs])
            if carry:
                s_t = s_scr[h]
                oi_scr[pl.ds(r0, seg), cs] = _dot_nt(qt_n, s_t.astype(BF16))
                s_scr[h] = dec * s_t + _dot_tn(v_n, kd_n)
            else:
                s = st_in_ref[n, h]
                oi_scr[pl.ds(r0, seg), cs] = _dot(qt_n, s.astype(BF16))
                st_out_ref[n, h] = _lane_column(dec) * s + _dot_tn(kd_n, v_n)
        return c

    lax.fori_loop(0, rows // seg, seg_body, 0, unroll=4)

    o = jnp.concatenate(
        [(lambda oh: oh * lax.rsqrt(jnp.mean(oh * oh, axis=-1, keepdims=True) + EPS))(
            ob_scr[h] + oi_scr[:, h * LANES:(h + 1) * LANES]) for h in range(heads)], axis=1)
    o_ref[...] = o * on_ref[...] * _sigmoid(g_ref[...])

    if carry:
        @pl.when(pl.program_id(1) == pl.num_programs(1) - 1)
        def _():
            for h in range(heads):
                st_out_ref[h] = s_scr[h].T


def _ssd_kernel(*refs, seg, carry, n_alias):
    z_ref, xa_ref, xb_ref, dt_ref, cw_ref, cb_ref, hp_ref, dsk_ref, ng_ref = refs[:9]
    st_in_ref, cs_in_ref = (None, None) if carry else refs[9:11]
    rest = refs[(9 if carry else 11) + n_alias:]
    o_ref, st_out_ref, cs_out_ref = rest[:3]
    scratch = rest[3:]
    h_scr = None
    if carry:
        h_scr, *scratch = scratch
    xpad, cm_scr, bm_scr, xw_scr, cle_scr, yi_scr = scratch

    rows, dinner = z_ref.shape
    cdim = xa_ref.shape[1] + xb_ref.shape[1]
    nseg = rows // seg
    gn = (cdim - dinner) // 2
    head_pad = xpad.shape[1] - seg
    tail = CONV_W - 1

    if carry:
        @pl.when(pl.program_id(1) == 0)
        def _():
            xpad[:, pl.ds(head_pad - tail, tail), :] = jnp.zeros((nseg, tail, cdim), F32)
    else:
        xpad[:, pl.ds(head_pad - tail, tail), :] = cs_in_ref[...]
    xpad[:, pl.ds(head_pad, seg), pl.ds(0, dinner)] = xa_ref[...].reshape(nseg, seg, dinner)
    xpad[:, pl.ds(head_pad, seg), pl.ds(dinner, cdim - dinner)] = xb_ref[...].reshape(nseg, seg, cdim - dinner)
    cw = cw_ref[...]
    conv = cb_ref[...] + xpad[:, pl.ds(head_pad - tail, seg), :] * cw[0:1, :]
    for j in range(1, CONV_W):
        conv = conv + xpad[:, pl.ds(head_pad - tail + j, seg), :] * cw[j:j + 1, :]
    new_tail = xpad[:, pl.ds(head_pad + seg - tail, tail), :]
    cs_out_ref[...] = new_tail.reshape(cs_out_ref.shape)
    if carry:
        xpad[:, pl.ds(head_pad - tail, tail), :] = new_tail
    u = conv.reshape(rows, cdim)
    u = u * _sigmoid(u)
    xs = u[:, :dinner]
    bm = u[:, dinner:dinner + gn]
    cm = u[:, dinner + gn:]

    hp = hp_ref[...]
    dt = _softplus(dt_ref[...] + hp[0:1, :])
    a = dt * (-jnp.exp(hp[1:2, :]))
    same, tri = _segment_masks(rows, seg)
    a3 = _split3(a)
    cum = _dot3(_onehot_bf16(tri), a3)
    cl = _dot3(_onehot_bf16(same), a3)
    cum3 = _split3(cum)
    dt3 = _split3(dt)
    hh = lax.broadcasted_iota(jnp.int32, (BF16_ROWS, LANES), 0)
    hl = lax.broadcasted_iota(jnp.int32, (BF16_ROWS, LANES), 1)
    eye_b = _onehot_bf16(hh == hl)
    cum_t = _dot_nt(eye_b, cum3[0]) + _dot_nt(eye_b, cum3[1]) + _dot_nt(eye_b, cum3[2])
    dt_t = _dot_nt(eye_b, dt3[0]) + _dot_nt(eye_b, dt3[1]) + _dot_nt(eye_b, dt3[2])

    eh = lax.broadcasted_iota(jnp.int32, (LANES, dinner), 0)
    el = lax.broadcasted_iota(jnp.int32, (LANES, dinner), 1)
    spread_b = _onehot_bf16(lax.shift_right_logical(el, SSM_HEAD_DIM.bit_length() - 1) == eh)
    cum_e = _dot3_r(cum3, spread_b)
    cl_e = _dot3_r(_split3(cl), spread_b)
    dt_e = _dot3_r(dt3, spread_b)
    xw_scr[...] = xs * (jnp.exp(cl_e - cum_e) * dt_e)
    cle_scr[...] = cl_e
    cm_scr[...] = cm
    bm_scr[...] = bm

    lane = lax.broadcasted_iota(jnp.int32, (rows, LANES), 1)
    heads_per_group = SSM_HEADS // SSM_GROUPS
    y_parts = []
    for g in range(SSM_GROUPS):
        gs = slice(g * SSM_STATE, (g + 1) * SSM_STATE)
        scores = _dot_nt(cm[:, gs].astype(BF16), bm[:, gs].astype(BF16))
        for j in range(g * heads_per_group // 2, (g + 1) * heads_per_group // 2):
            xp = xs[:, j * LANES:(j + 1) * LANES]
            yp = jnp.zeros((rows, LANES), F32)
            for half in range(2):
                h = 2 * j + half
                decay = jnp.where(tri, jnp.exp(cum[:, h:h + 1] - cum_t[h:h + 1, :]), 0.0)
                gm = (scores * decay * dt_t[h:h + 1, :]).astype(BF16)
                in_half = (lane < SSM_HEAD_DIM) if half == 0 else (lane >= SSM_HEAD_DIM)
                yp = yp + _dot(gm, jnp.where(in_half, xp, 0.0).astype(BF16))
            y_parts.append(yp)
    y = jnp.concatenate(y_parts, axis=1)

    if carry:
        @pl.when(pl.program_id(1) == 0)
        def _():
            h_scr[...] = jnp.zeros_like(h_scr)

    def seg_body(n, c):
        r0 = pl.multiple_of(n * seg, seg)
        for j in range(SSM_PAIRS):
            g = (2 * j) // heads_per_group
            gs = slice(g * SSM_STATE, (g + 1) * SSM_STATE)
            ps = slice(j * LANES, (j + 1) * LANES)
            hst = h_scr[j] if carry else st_in_ref[n, j]
            cm_n = cm_scr[pl.ds(r0, seg), gs].astype(BF16)
            yi_scr[pl.ds(r0, seg), ps] = _dot_nt(cm_n, hst.astype(BF16))
            xw_n = xw_scr[pl.ds(r0, seg), ps].astype(BF16)
            bm_n = bm_scr[pl.ds(r0, seg), gs].astype(BF16)
            upd = _dot_tn(xw_n, bm_n)
            h_new = _lane_column(jnp.exp(cle_scr[pl.ds(r0, 1), ps])) * hst + upd
            if carry:
                h_scr[j] = h_new
            else:
                st_out_ref[n, j] = h_new
        return c

    lax.fori_loop(0, nseg, seg_body, 0, unroll=min(2, nseg))

    y = y + jnp.exp(cum_e) * yi_scr[...] + dsk_ref[...] * xs
    z = z_ref[...]
    y = y * (z * _sigmoid(z))
    gw = dinner // SSM_GROUPS
    y = jnp.concatenate(
        [(lambda yg: yg * lax.rsqrt(jnp.mean(yg * yg, axis=-1, keepdims=True) + EPS))(
            y[:, g * gw:(g + 1) * gw]) for g in range(SSM_GROUPS)], axis=1)
    o_ref[...] = y * ng_ref[...]

    if carry:
        @pl.when(pl.program_id(1) == pl.num_programs(1) - 1)
        def _():
            st_out_ref[...] = h_scr[...]


def _dot3_r(xs, m_b):
    return _dot(xs[0], m_b) + _dot(xs[1], m_b) + _dot(xs[2], m_b)


def _attn_kernel(q_ref, k_ref, v_ref, x_ref, wo_ref, g_ref, *rest, seg):
    o_ref, att_scr = rest[-2:]
    rows = q_ref.shape[0]
    scale = 1.0 / math.sqrt(X_HEAD_DIM)

    def seg_body(n, c):
        r0 = pl.multiple_of(n * seg, seg)
        for h in range(X_HEADS):
            cs = slice(h * X_HEAD_DIM, (h + 1) * X_HEAD_DIM)
            qh = q_ref[pl.ds(r0, seg), cs].astype(BF16)
            s = _dot_nt(qh, k_ref[n, :, cs].astype(BF16)) * scale
            e = jnp.exp(s - jnp.max(s, axis=-1, keepdims=True))
            p = e / jnp.sum(e, axis=-1, keepdims=True)
            att_scr[pl.ds(r0, seg), cs] = _dot(p.astype(BF16), v_ref[n, :, cs].astype(BF16))
        return c

    lax.fori_loop(0, rows // seg, seg_body, 0)
    att = _dot(att_scr[...].astype(BF16), wo_ref[...])
    o_ref[...] = x_ref[...] + _rms(att, g_ref[...])


def _attn_interleaved_kernel(q_ref, k_ref, v_ref, x_ref, wo_ref, g_ref, *rest, seg):
    o_ref, att_scr = rest[-2:]
    rows = q_ref.shape[0]
    nkv = k_ref.shape[1]
    scale = 1.0 / math.sqrt(X_HEAD_DIM)
    rr = lax.broadcasted_iota(jnp.int32, (X_HEADS * seg, nkv), 0)
    cc = lax.broadcasted_iota(jnp.int32, (X_HEADS * seg, nkv), 1)
    own_head = (cc & (X_HEADS - 1)) == lax.shift_right_logical(rr, seg.bit_length() - 1)

    def seg_body(n, c):
        r0 = pl.multiple_of(n * seg, seg)
        qs = jnp.concatenate(
            [q_ref[pl.ds(r0, seg), h * X_HEAD_DIM:(h + 1) * X_HEAD_DIM] for h in range(X_HEADS)],
            axis=0).astype(BF16)
        s = jnp.where(own_head, _dot_nt(qs, k_ref[n].astype(BF16)) * scale, -jnp.inf)
        e = jnp.exp(s - jnp.max(s, axis=-1, keepdims=True))
        p = e / jnp.sum(e, axis=-1, keepdims=True)
        o = _dot(p.astype(BF16), v_ref[n].astype(BF16))
        for h in range(X_HEADS):
            att_scr[pl.ds(r0, seg), h * X_HEAD_DIM:(h + 1) * X_HEAD_DIM] = o[h * seg:(h + 1) * seg]
        return c

    lax.fori_loop(0, rows // seg, seg_body, 0, unroll=2)
    att = _dot(att_scr[...].astype(BF16), wo_ref[...])
    o_ref[...] = x_ref[...] + _rms(att, g_ref[...])


def _params(n_grid):
    return pltpu.CompilerParams(dimension_semantics=("arbitrary",) * n_grid,
                                vmem_limit_bytes=VMEM_LIMIT)


_ANY = pl.BlockSpec(memory_space=pl.ANY)


def _vec3(p):
    return p.reshape(p.shape[0], 1, p.shape[1])


def _layer_row(n):
    return lambda layer: pl.BlockSpec((None, 1, n), lambda *_: (layer, 0, 0))


def _in_proj(x, g3, w_in_t, layer, *, tm, tn):
    m, d = x.shape
    n_all = w_in_t.shape[1]
    n_main = (n_all // LANES) * LANES
    n_tail = n_all - n_main
    return pl.pallas_call(
        functools.partial(_in_proj_kernel, n_tail=n_tail),
        grid=(m // tm, n_main // tn),
        in_specs=[
            pl.BlockSpec((tm, d), lambda i, j: (i, 0)),
            pl.BlockSpec((None, 1, d), lambda i, j: (layer, 0, 0)),
            pl.BlockSpec((None, tn, d), lambda i, j: (layer, j, 0)),
            pl.BlockSpec((None, LANES, d), lambda i, j: (layer, n_main // LANES, 0)),
        ],
        out_specs=[
            pl.BlockSpec((tm, tn), lambda i, j: (i, j)),
            pl.BlockSpec((tm, LANES), lambda i, j: (i, 0)),
        ],
        out_shape=[jax.ShapeDtypeStruct((m, n_main), F32), jax.ShapeDtypeStruct((m, LANES), F32)],
        scratch_shapes=[pltpu.VMEM((tm, d), BF16)],
        compiler_params=_params(2),
        name=f"in_proj_l{layer}",
    )(x, g3, w_in_t, w_in_t)


def _rms_matmul(x, g3, w, layer, *, tm, tn, name, stacked_prev=None, stacked=False):
    m, d = x.shape
    depth, _, n = w.shape
    in_specs = [
        pl.BlockSpec((tm, d), lambda i, j: (i, 0)),
        pl.BlockSpec((None, 1, d), lambda i, j: (layer, 0, 0)),
        pl.BlockSpec((None, d, tn), lambda i, j: (layer, 0, j)),
    ]
    args = [x, g3, w]
    aliases = {}
    if stacked:
        out_spec = pl.BlockSpec((None, tm, tn), lambda i, j: (layer, i, j))
        out_shape = jax.ShapeDtypeStruct((depth, m, n), F32)
        if stacked_prev is not None:
            in_specs.append(_ANY)
            args.append(stacked_prev)
            aliases = {3: 0}
    else:
        out_spec = pl.BlockSpec((tm, tn), lambda i, j: (i, j))
        out_shape = jax.ShapeDtypeStruct((m, n), F32)

    def body(x_ref, g_ref, w_ref, *rest):
        o_ref, xn_ref = rest[-2:]
        _rms_matmul_kernel(x_ref, g_ref, w_ref, o_ref, xn_ref)

    return pl.pallas_call(
        body,
        grid=(m // tm, n // tn),
        in_specs=in_specs,
        out_specs=out_spec,
        out_shape=out_shape,
        scratch_shapes=[pltpu.VMEM((tm, d), BF16)],
        input_output_aliases=aliases,
        compiler_params=_params(2),
        name=f"{name}_l{layer}",
    )(*args)


def _out_proj(a, w_b, x, g3, gq3, wq_b, layer, *, tm):
    m, kdim = a.shape
    n = w_b.shape[2]
    nq = wq_b.shape[2]
    return pl.pallas_call(
        _out_proj_kernel,
        grid=(m // tm,),
        in_specs=[
            pl.BlockSpec((tm, kdim), lambda i: (i, 0)),
            pl.BlockSpec((None, kdim, n), lambda i: (layer, 0, 0)),
            pl.BlockSpec((tm, n), lambda i: (i, 0)),
            pl.BlockSpec((None, 1, n), lambda i: (layer, 0, 0)),
            pl.BlockSpec((None, 1, n), lambda i: (layer, 0, 0)),
            pl.BlockSpec((None, n, nq), lambda i: (layer, 0, 0)),
        ],
        out_specs=[pl.BlockSpec((tm, n), lambda i: (i, 0)),
                   pl.BlockSpec((tm, nq), lambda i: (i, 0))],
        out_shape=[jax.ShapeDtypeStruct((m, n), F32), jax.ShapeDtypeStruct((m, nq), F32)],
        compiler_params=_params(1),
        name=f"out_proj_l{layer}",
    )(a, w_b, x, g3, gq3, wq_b)


def _mlp(x, g1, w1, w2, g2, layer, *, tm, tf):
    m, d = x.shape
    dff = w1.shape[2]
    single = pl.Buffered(1)
    return pl.pallas_call(
        functools.partial(_mlp_kernel, n_chunk=4),
        grid=(m // tm, dff // tf),
        in_specs=[
            pl.BlockSpec((tm, d), lambda i, f: (i, 0), pipeline_mode=single),
            pl.BlockSpec((None, 1, d), lambda i, f: (layer, 0, 0)),
            pl.BlockSpec((None, d, tf), lambda i, f: (layer, 0, f)),
            pl.BlockSpec((None, tf, d), lambda i, f: (layer, f, 0)),
            pl.BlockSpec((None, 1, d), lambda i, f: (layer, 0, 0)),
        ],
        out_specs=pl.BlockSpec((tm, d), lambda i, f: (i, 0)),
        out_shape=jax.ShapeDtypeStruct((m, d), F32),
        scratch_shapes=[pltpu.VMEM((tm, d), BF16)],
        compiler_params=_params(2),
        name=f"mlp_l{layer}",
    )(x, g1, w1, w2, g2)


def _final_rms(x, g2, row0, nrows, *, tm, name):
    d = x.shape[1]
    return pl.pallas_call(
        _final_rms_kernel,
        grid=(nrows // tm,),
        in_specs=[
            pl.BlockSpec((tm, d), lambda i: (row0 // tm + i, 0)),
            pl.BlockSpec((1, d), lambda i: (0, 0)),
        ],
        out_specs=pl.BlockSpec((tm, d), lambda i: (i, 0)),
        out_shape=jax.ShapeDtypeStruct((nrows, d), F32),
        compiler_params=_params(1),
        name=name,
    )(x, g2)


def _hgrn(proj, lb3, on3, layer, *, row0, nseq, seqlen, rows, seg, mix_shape, mix_prev, st_in, st_prev,
          st_shape, name):
    carry = st_in is None
    width = HG_HEADS * HG_DIM
    depth = lb3.shape[0]
    if carry:
        nt = seqlen // rows
        grid = (nseq, nt)
        rmap = lambda b, t: row0 // rows + b * nt + t
        st_block = (None, None, HG_HEADS, HG_DIM, HG_DIM)
        st_map = lambda b, t: (layer, b, 0, 0, 0)
    else:
        per = rows // seg
        grid = (nseq // per, 1)
        rmap = lambda s, t: row0 // rows + s
        st_block = (None, per, HG_HEADS, HG_DIM, HG_DIM)
        st_map = lambda s, t: (layer, s, 0, 0, 0)

    col = lambda c: pl.BlockSpec((rows, width), lambda *ix: (rmap(*ix), c))
    in_specs = [col(0), col(1), col(2), col(3),
                pl.BlockSpec((depth, 1, width), lambda *ix: (0, 0, 0)),
                pl.BlockSpec((None, 1, width), lambda *ix: (layer, 0, 0))]
    args = [proj, proj, proj, proj, lb3, on3]
    if not carry:
        in_specs.append(pl.BlockSpec(st_block, st_map))
        args.append(st_in)
    n_fixed = len(args)
    aliases = {}
    if mix_prev is not None:
        aliases[len(args)] = 0
        in_specs.append(_ANY)
        args.append(mix_prev)
    if st_prev is not None:
        aliases[len(args)] = 1
        in_specs.append(_ANY)
        args.append(st_prev)

    tile = pltpu.VMEM((rows, width), F32)
    per_head = pltpu.VMEM((HG_HEADS, rows, HG_DIM), F32)
    scratch = [tile] * 5 + [per_head] * 5
    if carry:
        scratch = [pltpu.VMEM((HG_HEADS, HG_DIM, HG_DIM), F32)] + scratch
    return pl.pallas_call(
        functools.partial(_hgrn_kernel, layer=layer, seg=seg, carry=carry, n_alias=len(args) - n_fixed),
        grid=grid,
        in_specs=in_specs,
        out_specs=[pl.BlockSpec((rows, width), lambda *ix: (rmap(*ix), 0)),
                   pl.BlockSpec(st_block, st_map)],
        out_shape=[jax.ShapeDtypeStruct(mix_shape, F32), jax.ShapeDtypeStruct(st_shape, F32)],
        scratch_shapes=scratch,
        input_output_aliases=aliases,
        compiler_params=_params(2),
        name=f"{name}_l{layer}",
    )(*args)


def _ssd(proj, dt, cw, cb3, hp, dsk3, ng3, layer, *, row0, nseq, seqlen, rows, seg, mix_prev, st_in, cs_in,
         st_prev, cs_prev, st_shape, cs_shape, name):
    carry = st_in is None
    dinner = SSM_HEADS * SSM_HEAD_DIM
    cdim = dinner + 2 * SSM_GROUPS * SSM_STATE
    assert (4 * HG_HEADS * HG_DIM) % dinner == 0 and cdim == 2 * dinner
    z_col = (4 * HG_HEADS * HG_DIM) // dinner
    tail = CONV_W - 1
    if carry:
        nt = seqlen // rows
        grid = (nseq, nt)
        rmap = lambda b, t: row0 // rows + b * nt + t
        st_block = (None, None, SSM_PAIRS, LANES, SSM_STATE)
        cs_block = (None, None, tail, cdim)
        st_map = lambda b, t: (layer, b, 0, 0, 0)
        cs_map = lambda b, t: (layer, b, 0, 0)
    else:
        per = rows // seg
        grid = (nseq // per, 1)
        rmap = lambda s, t: row0 // rows + s
        st_block = (None, per, SSM_PAIRS, LANES, SSM_STATE)
        cs_block = (None, per, tail, cdim)
        st_map = lambda s, t: (layer, s, 0, 0, 0)
        cs_map = lambda s, t: (layer, s, 0, 0)

    in_specs = [
        pl.BlockSpec((rows, dinner), lambda *ix: (rmap(*ix), z_col)),
        pl.BlockSpec((rows, dinner), lambda *ix: (rmap(*ix), z_col + 1)),
        pl.BlockSpec((rows, dinner), lambda *ix: (rmap(*ix), z_col + 2)),
        pl.BlockSpec((rows, LANES), lambda *ix: (rmap(*ix), 0)),
        pl.BlockSpec((None, CONV_W, cdim), lambda *ix: (layer, 0, 0)),
        pl.BlockSpec((None, 1, cdim), lambda *ix: (layer, 0, 0)),
        pl.BlockSpec((None, 8, LANES), lambda *ix: (layer, 0, 0)),
        pl.BlockSpec((None, 1, dinner), lambda *ix: (layer, 0, 0)),
        pl.BlockSpec((None, 1, dinner), lambda *ix: (layer, 0, 0)),
    ]
    args = [proj, proj, proj, dt, cw, cb3, hp, dsk3, ng3]
    if not carry:
        in_specs += [pl.BlockSpec(st_block, st_map), pl.BlockSpec(cs_block, cs_map)]
        args += [st_in, cs_in]
    n_fixed = len(args)
    aliases = {len(args): 0}
    in_specs.append(_ANY)
    args.append(mix_prev)
    if st_prev is not None:
        aliases[len(args)] = 1
        aliases[len(args) + 1] = 2
        in_specs += [_ANY, _ANY]
        args += [st_prev, cs_prev]

    nseg = rows // seg
    scratch = [pltpu.VMEM((nseg, seg + 8, cdim), F32),
               pltpu.VMEM((rows, SSM_GROUPS * SSM_STATE), F32),
               pltpu.VMEM((rows, SSM_GROUPS * SSM_STATE), F32),
               pltpu.VMEM((rows, dinner), F32),
               pltpu.VMEM((rows, dinner), F32),
               pltpu.VMEM((rows, dinner), F32)]
    if carry:
        scratch = [pltpu.VMEM((SSM_PAIRS, LANES, SSM_STATE), F32)] + scratch
    mix_shape = mix_prev.shape
    return pl.pallas_call(
        functools.partial(_ssd_kernel, seg=seg, carry=carry, n_alias=len(args) - n_fixed),
        grid=grid,
        in_specs=in_specs,
        out_specs=[pl.BlockSpec((rows, dinner), lambda *ix: (rmap(*ix), HG_HEADS * HG_DIM // dinner)),
                   pl.BlockSpec(st_block, st_map),
                   pl.BlockSpec(cs_block, cs_map)],
        out_shape=[jax.ShapeDtypeStruct(mix_shape, F32), jax.ShapeDtypeStruct(st_shape, F32),
                   jax.ShapeDtypeStruct(cs_shape, F32)],
        scratch_shapes=scratch,
        input_output_aliases=aliases,
        compiler_params=_params(2),
        name=f"{name}_l{layer}",
    )(*args)


def _attn(q, mk, mv, x, wo, g3, layer, *, row0, nrows, seqlen, rows, seg, x_prev, name):
    m, d = x.shape
    xdim = q.shape[1]
    interleaved = mk.shape[3] == X_HEAD_DIM
    body = _attn_interleaved_kernel if interleaved else _attn_kernel
    per = rows // seg
    base = row0 // rows
    assert seg == min(rows, seqlen) and seqlen % seg == 0
    tiles_per_seq = seqlen // seg if per == 1 else 1
    kv_map = lambda i: (layer, i // tiles_per_seq, 0, 0)
    in_specs = [
        pl.BlockSpec((rows, xdim), lambda i: (base + i, 0)),
        pl.BlockSpec((None, per) + mk.shape[2:], kv_map),
        pl.BlockSpec((None, per) + mv.shape[2:], kv_map),
        pl.BlockSpec((rows, d), lambda i: (base + i, 0)),
        pl.BlockSpec((None, xdim, d), lambda i: (layer, 0, 0)),
        pl.BlockSpec((None, 1, d), lambda i: (layer, 0, 0)),
    ]
    args = [q, mk, mv, x, wo, g3]
    aliases = {}
    if x_prev is not None:
        aliases = {len(args): 0}
        in_specs.append(_ANY)
        args.append(x_prev)
    return pl.pallas_call(
        functools.partial(body, seg=seg),
        grid=(nrows // rows,),
        in_specs=in_specs,
        out_specs=pl.BlockSpec((rows, d), lambda i: (base + i, 0)),
        out_shape=jax.ShapeDtypeStruct((m, d), F32),
        scratch_shapes=[pltpu.VMEM((rows, xdim), F32)],
        input_output_aliases=aliases,
        compiler_params=_params(1),
        name=f"{name}_l{layer}",
    )(*args)


def kernel(x_prompt, x_sample, mem_prompt, state_hgrn, state_ssm, state_conv, cache_mem_k, cache_mem_v, g_pre_mix, g_post_mix, g_pre_x, g_post_x, g_pre_mlp, g_post_mlp, g_mem, g_final, w_in, hg_lb, hg_onorm, conv_w, conv_b, dt_bias, a_log, d_skip, ssm_norm, w_out, wq_x, wk_x, wv_x, wo_x, w_mlp1, w_mlp2):
    bp, lp, d = x_prompt.shape
    bs, ls, _ = x_sample.shape
    depth = w_in.shape[0]
    n_mem = mem_prompt.shape[1]
    mp, ms = bp * lp, bs * ls
    m = mp + ms
    xdim = X_HEADS * X_HEAD_DIM
    hg_width = HG_HEADS * HG_DIM
    dinner = SSM_HEADS * SSM_HEAD_DIM
    cdim = conv_w.shape[2]
    mix_shape = (m, hg_width + dinner)

    x = jnp.concatenate([x_prompt.reshape(mp, d), x_sample.reshape(ms, d)], axis=0)
    mem = mem_prompt.reshape(bp * n_mem, d)

    g_pre_mix3, g_post_mix3, g_pre_x3, g_post_x3 = map(_vec3, (g_pre_mix, g_post_mix, g_pre_x, g_post_x))
    g_pre_mlp3, g_post_mlp3, g_mem3 = map(_vec3, (g_pre_mlp, g_post_mlp, g_mem))
    lb3, on3, cb3, ng3 = map(_vec3, (hg_lb, hg_onorm, conv_b, ssm_norm))
    dsk3 = _vec3(jnp.repeat(d_skip, SSM_HEAD_DIM, axis=1))
    hp = jnp.zeros((depth, 8, LANES), F32)
    hp = hp.at[:, 0, :SSM_HEADS].set(dt_bias).at[:, 1, :SSM_HEADS].set(a_log)

    st_ssm_in = state_ssm.reshape(depth, bs, SSM_PAIRS, LANES, SSM_STATE)
    ck = cache_mem_k.reshape(depth, bs, n_mem * X_HEADS, X_HEAD_DIM)
    cv = cache_mem_v.reshape(depth, bs, n_mem * X_HEADS, X_HEAD_DIM)

    w_in_t = jnp.swapaxes(w_in, 1, 2).astype(BF16)
    w_out_b, wq_b, wo_b = w_out.astype(BF16), wq_x.astype(BF16), wo_x.astype(BF16)

    p_hg = p_ssm = p_conv = p_mk = p_mv = s_hg = s_ssm = s_conv = None
    p_hg_shape = (depth, bp, HG_HEADS, HG_DIM, HG_DIM)
    s_hg_shape = (depth, bs, HG_HEADS, HG_DIM, HG_DIM)
    p_ssm_shape = (depth, bp, SSM_PAIRS, LANES, SSM_STATE)
    s_ssm_shape = (depth, bs, SSM_PAIRS, LANES, SSM_STATE)
    p_conv_shape = (depth, bp, CONV_W - 1, cdim)
    s_conv_shape = (depth, bs, CONV_W - 1, cdim)

    tile = 128
    hg_seg = 16

    for layer in range(depth):
        proj, dt = _in_proj(x, g_pre_mix3, w_in_t, layer, tm=1024, tn=1024)

        mix, p_hg = _hgrn(proj, lb3, on3, layer, row0=0, nseq=bp, seqlen=lp, rows=tile, seg=hg_seg,
                          mix_shape=mix_shape, mix_prev=None, st_in=None, st_prev=p_hg,
                          st_shape=p_hg_shape, name="hgrn_prompt")
        mix, s_hg = _hgrn(proj, lb3, on3, layer, row0=mp, nseq=bs, seqlen=ls, rows=tile, seg=ls,
                          mix_shape=mix_shape, mix_prev=mix, st_in=state_hgrn, st_prev=s_hg,
                          st_shape=s_hg_shape, name="hgrn_sample")
        mix, p_ssm, p_conv = _ssd(proj, dt, conv_w, cb3, hp, dsk3, ng3, layer, row0=0, nseq=bp, seqlen=lp,
                                  rows=tile, seg=tile, mix_prev=mix, st_in=None, cs_in=None,
                                  st_prev=p_ssm, cs_prev=p_conv, st_shape=p_ssm_shape,
                                  cs_shape=p_conv_shape, name="ssd_prompt")
        mix, s_ssm, s_conv = _ssd(proj, dt, conv_w, cb3, hp, dsk3, ng3, layer, row0=mp, nseq=bs, seqlen=ls,
                                  rows=tile, seg=ls, mix_prev=mix, st_in=st_ssm_in, cs_in=state_conv,
                                  st_prev=s_ssm, cs_prev=s_conv, st_shape=s_ssm_shape,
                                  cs_shape=s_conv_shape, name="ssd_sample")

        x, q = _out_proj(mix, w_out_b, x, g_post_mix3, g_pre_x3, wq_b, layer, tm=512)
        p_mk = _rms_matmul(mem, g_mem3, wk_x, layer, tm=bp * n_mem, tn=xdim, name="mem_k",
                           stacked=True, stacked_prev=p_mk)
        p_mv = _rms_matmul(mem, g_mem3, wv_x, layer, tm=bp * n_mem, tn=xdim, name="mem_v",
                           stacked=True, stacked_prev=p_mv)
        mk4 = p_mk.reshape(depth, bp, n_mem, xdim)
        mv4 = p_mv.reshape(depth, bp, n_mem, xdim)
        x2 = _attn(q, mk4, mv4, x, wo_b, g_post_x3, layer, row0=0, nrows=mp, seqlen=lp, rows=512, seg=512,
                   x_prev=None, name="attn_prompt")
        x = _attn(q, ck, cv, x, wo_b, g_post_x3, layer, row0=mp, nrows=ms, seqlen=ls, rows=tile, seg=ls,
                  x_prev=x2, name="attn_sample")

        x = _mlp(x, g_pre_mlp3, w_mlp1, w_mlp2, g_post_mlp3, layer, tm=1024, tf=512)

    g_fin = g_final.reshape(1, d)
    y_prompt = _final_rms(x, g_fin, 0, mp, tm=512, name="final_prompt").reshape(bp, lp, d)
    y_sample = _final_rms(x, g_fin, mp, ms, tm=512, name="final_sample").reshape(bs, ls, d)

    return (y_prompt, y_sample,
            p_hg,
            p_ssm.reshape(depth, bp, SSM_HEADS, SSM_HEAD_DIM, SSM_STATE),
            p_conv,
            p_mk.reshape(depth, bp, n_mem, X_HEADS, X_HEAD_DIM),
            p_mv.reshape(depth, bp, n_mem, X_HEADS, X_HEAD_DIM),
            s_hg,
            s_ssm.reshape(depth, bs, SSM_HEADS, SSM_HEAD_DIM, SSM_STATE),
            s_conv)
```
